```python
import jax, jax.numpy as jnp
from jax import lax
import numpy as np

D_MODEL = 1024
BATCH = 8
SEQ = 4096
DEPTH = 1

GRID_W = 64
CTX_LEN = 256

F_GROUP_DIM = 64
F_DIM = D_MODEL // 2
F_GROUPS = F_DIM // F_GROUP_DIM
SSM_HEAD_DIM = 64
D_SSM = 3 * D_MODEL // 2
SSM_HEADS = D_SSM // SSM_HEAD_DIM
SSM_GROUPS = 4
D_STATE = 128
CONV_K = 5
CHUNK = 128
D_MIX = F_DIM + D_SSM
CONV_DIM = D_SSM + 2 * SSM_GROUPS * D_STATE
D_IN_PROJ = F_DIM + D_SSM + CONV_DIM + 2 * SSM_HEADS

N_EXPERTS = 16
CAPACITY_FACTOR = 2
D_EXPERT = 2048
N_MOD = 6
EPS = 1e-6

kernel_name = 'hybrid_fnet_ssd_ecmoe_diffusion'


def rms_norm(x, w):
    xf = x.astype(jnp.float32)
    y = xf * lax.rsqrt(jnp.mean(xf * xf, axis=-1, keepdims=True) + EPS)
    return (y * w.astype(jnp.float32)).astype(x.dtype)


def modulate(h, shift, scale):
    return h * (1 + scale) + shift


def centred_dwconv(u, w, b):
    out = lax.conv_general_dilated(
        u, w[:, None, :].astype(u.dtype), window_strides=(1,),
        padding=[(CONV_K // 2, CONV_K // 2)],
        dimension_numbers=('NWC', 'WIO', 'NWC'),
        feature_group_count=u.shape[-1])
    return out + b


def mixer_inputs(h, w_in, conv_w, conv_b, dt_bias):
    bsz, L, _ = h.shape
    proj = h @ w_in
    u_f, z, xbc, dt_raw = jnp.split(proj, [F_DIM, F_DIM + D_SSM, F_DIM + D_SSM + CONV_DIM], axis=-1)
    xbc = jax.nn.silu(centred_dwconv(xbc, conv_w, conv_b))
    xs, bm, cm = jnp.split(xbc, [D_SSM, D_SSM + SSM_GROUPS * D_STATE], axis=-1)
    xs = xs.reshape(bsz, L, SSM_HEADS, SSM_HEAD_DIM)
    bm = bm.reshape(bsz, L, SSM_GROUPS, D_STATE)
    cm = cm.reshape(bsz, L, SSM_GROUPS, D_STATE)
    dt = jax.nn.softplus(dt_raw.astype(jnp.float32).reshape(bsz, L, 2, SSM_HEADS)
                         + dt_bias.astype(jnp.float32))
    return u_f, z, xs, bm, cm, dt[:, :, 0], dt[:, :, 1]


def ssd_chunked(xs, dt, a, bm, cm, h0, with_output):
    bsz, L, H, P = xs.shape
    G, N = bm.shape[2], bm.shape[3]
    HG = H // G
    nc = L // CHUNK
    xs = xs.astype(jnp.float32)
    bm = bm.astype(jnp.float32).reshape(bsz, nc, CHUNK, G, N)
    cm = cm.astype(jnp.float32).reshape(bsz, nc, CHUNK, G, N)
    xdt = (xs * dt[..., None]).reshape(bsz, nc, CHUNK, G, HG, P)
    a_cs = jnp.cumsum((dt * a).reshape(bsz, nc, CHUNK, G, HG), axis=2)
    to_end = jnp.exp(a_cs[:, :, -1:] - a_cs)
    states = jnp.einsum('bclgn,bclgh,bclghp->bcghpn', bm, to_end, xdt)
    chunk_decay = jnp.exp(a_cs[:, :, -1])

    def step(h, inp):
        st, dec = inp
        return dec[..., None, None] * h + st, h

    h_last, h_prev = lax.scan(step, h0.astype(jnp.float32).reshape(bsz, G, HG, P, N),
                              (jnp.moveaxis(states, 1, 0), jnp.moveaxis(chunk_decay, 1, 0)))
    h_last = h_last.reshape(bsz, H, P, N)
    if not with_output:
        return None, h_last
    h_prev = jnp.moveaxis(h_prev, 0, 1)
    lower = jnp.tril(jnp.ones((CHUNK, CHUNK), dtype=bool))
    seg = a_cs[:, :, :, None] - a_cs[:, :, None, :]
    decay = jnp.exp(jnp.where(lower[:, :, None, None], seg, -jnp.inf))
    cb = jnp.einsum('bclgn,bcsgn->bclsg', cm, bm)
    y_diag = jnp.einsum('bclsg,bclsgh,bcsghp->bclghp', cb, decay, xdt)
    y_off = jnp.einsum('bclgn,bcghpn,bclgh->bclghp', cm, h_prev, jnp.exp(a_cs))
    return (y_diag + y_off).reshape(bsz, L, H, P), h_last


def ssd_bidir(xs, bm, cm, dt_f, dt_b, a_log, h0_f, h0_b, with_output):
    a = -jnp.exp(a_log.astype(jnp.float32))
    flip = lambda t: jnp.flip(t, axis=1)
    y_f, h_f = ssd_chunked(xs, dt_f, a[0], bm, cm, h0_f, with_output)
    y_b, h_b = ssd_chunked(flip(xs), flip(dt_b), a[1], flip(bm), flip(cm), h0_b, with_output)
    y = y_f + flip(y_b) if with_output else None
    return y, h_f, h_b


def fourier_mix(u_f, w_fourier):
    bsz, L, _ = u_f.shape
    u = u_f.astype(jnp.float32).reshape(bsz, L, F_GROUPS, F_GROUP_DIM)
    spec = jnp.fft.fft2(u, axes=(1, 3), norm='ortho').real
    out = jnp.einsum('blgi,gij->blgj', spec, w_fourier.astype(jnp.float32))
    return out.reshape(bsz, L, F_DIM).astype(u_f.dtype)


def mixer_out(u_f, y_ssm, xs, z, d_skip, ssm_norm_w, w_fourier, w_out):
    bsz, L, _ = z.shape
    y = y_ssm + d_skip.astype(jnp.float32)[:, None] * xs.astype(jnp.float32)
    y = y.reshape(bsz, L, D_SSM) * jax.nn.silu(z.astype(jnp.float32))
    yg = y.reshape(bsz, L, SSM_GROUPS, D_SSM // SSM_GROUPS)
    yg = yg * lax.rsqrt(jnp.mean(yg * yg, axis=-1, keepdims=True) + EPS)
    y = (yg.reshape(bsz, L, D_SSM) * ssm_norm_w.astype(jnp.float32)).astype(z.dtype)
    return jnp.concatenate([fourier_mix(u_f, w_fourier), y], axis=-1) @ w_out


def ec_moe(h, w_router, w_gate, w_up, w_down):
    bsz, T, _ = h.shape
    cap = CAPACITY_FACTOR * T // N_EXPERTS
    probs = jax.nn.softmax((h @ w_router).astype(jnp.float32), axis=-1)
    vals, idx = lax.top_k(jnp.swapaxes(probs, 1, 2), cap)
    bidx = jnp.arange(bsz)[:, None, None]
    xg = h[bidx, idx]
    g = jnp.einsum('becd,edf->becf', xg, w_gate)
    u = jnp.einsum('becd,edf->becf', xg, w_up)
    y = jnp.einsum('becf,efd->becd', jax.nn.silu(g) * u, w_down) * vals[..., None].astype(h.dtype)
    return jnp.zeros_like(h).at[bidx, idx].add(y)


def setup_inputs(seed: int = 0) -> dict:
    key = jax.random.key(seed)
    ks = jax.random.split(key, 24)
    nrm = lambda k, shape, s: jax.random.normal(k, shape, jnp.float32) * s
    dt0 = jnp.exp(jax.random.uniform(ks[9], (DEPTH, 2, SSM_HEADS), jnp.float32,
                                     np.log(1e-3).astype(np.float32), np.log(1e-1).astype(np.float32)))
    return {
        'x': nrm(ks[0], (BATCH, SEQ, D_MODEL), 1.0),
        'c': nrm(ks[1], (BATCH, D_MODEL), 1.0),
        'ctx': nrm(ks[2], (BATCH, CTX_LEN, D_MODEL), 1.0),
        'c_ctx': nrm(ks[3], (D_MODEL,), 1.0),
        'w_ada': nrm(ks[4], (DEPTH, D_MODEL, N_MOD * D_MODEL), 0.5 * D_MODEL ** -0.5),
        'b_ada': nrm(ks[5], (DEPTH, N_MOD * D_MODEL), 0.02),
        'norm1_w': 1.0 + nrm(ks[6], (DEPTH, D_MODEL), 0.05),
        'w_in': nrm(ks[7], (DEPTH, D_MODEL, D_IN_PROJ), D_MODEL ** -0.5),
        'conv_w': nrm(ks[8], (DEPTH, CONV_K, CONV_DIM), CONV_K ** -0.5),
        'conv_b': nrm(ks[10], (DEPTH, CONV_DIM), 0.02),
        'dt_bias': dt0 + jnp.log(-jnp.expm1(-dt0)),
        'a_log': jnp.log(jax.random.uniform(ks[11], (DEPTH, 2, SSM_HEADS), jnp.float32, 1.0, 16.0)),
        'd_skip': 1.0 + nrm(ks[12], (DEPTH, SSM_HEADS), 0.1),
        'ssm_norm_w': 1.0 + nrm(ks[13], (DEPTH, D_SSM), 0.05),
        'w_fourier': nrm(ks[14], (DEPTH, F_GROUPS, F_GROUP_DIM, F_GROUP_DIM), F_GROUP_DIM ** -0.5),
        'w_out': nrm(ks[15], (DEPTH, D_MIX, D_MODEL), D_MIX ** -0.5),
        'norm2_w': 1.0 + nrm(ks[16], (DEPTH, D_MODEL), 0.05),
        'w_router': nrm(ks[17], (DEPTH, D_MODEL, N_EXPERTS), D_MODEL ** -0.5),
        'w_gate': nrm(ks[18], (DEPTH, N_EXPERTS, D_MODEL, D_EXPERT), D_MODEL ** -0.5),
        'w_up': nrm(ks[19], (DEPTH, N_EXPERTS, D_MODEL, D_EXPERT), D_MODEL ** -0.5),
        'w_down': nrm(ks[20], (DEPTH, N_EXPERTS, D_EXPERT, D_MODEL), D_EXPERT ** -0.5),
        'final_norm_w': 1.0 + nrm(ks[21], (D_MODEL,), 0.05),
    }


def reference(x, c, ctx, c_ctx, w_ada, b_ada, norm1_w, w_in, conv_w, conv_b, dt_bias, a_log,
              d_skip, ssm_norm_w, w_fourier, w_out, norm2_w, w_router, w_gate, w_up, w_down,
              final_norm_w):
    bsz = x.shape[0]
    for layer in range(DEPTH):
        last = layer == DEPTH - 1
        mod_l = (jax.nn.silu(c) @ w_ada[layer] + b_ada[layer]).reshape(bsz, N_MOD, 1, D_MODEL)
        mod_c = (jax.nn.silu(c_ctx) @ w_ada[layer] + b_ada[layer]).reshape(N_MOD, D_MODEL)
        sh1, sc1, g1, sh2, sc2, g2 = [mod_l[:, i] for i in range(N_MOD)]
        csh1, csc1, cg1, csh2, csc2, cg2 = [mod_c[i] for i in range(N_MOD)]

        h_c = modulate(rms_norm(ctx, norm1_w[layer]), csh1, csc1)
        u_fc, z_c, xs_c, bm_c, cm_c, dtf_c, dtb_c = mixer_inputs(
            h_c, w_in[layer], conv_w[layer], conv_b[layer], dt_bias[layer])
        zeros = jnp.zeros((bsz, SSM_HEADS, SSM_HEAD_DIM, D_STATE), jnp.float32)
        y_c, h_f, h_b = ssd_bidir(xs_c, bm_c, cm_c, dtf_c, dtb_c, a_log[layer], zeros, zeros, not last)

        h_l = modulate(rms_norm(x, norm1_w[layer]), sh1, sc1)
        u_fl, z_l, xs_l, bm_l, cm_l, dtf_l, dtb_l = mixer_inputs(
            h_l, w_in[layer], conv_w[layer], conv_b[layer], dt_bias[layer])
        y_l, _, _ = ssd_bidir(xs_l, bm_l, cm_l, dtf_l, dtb_l, a_log[layer], h_f, h_b, True)
        x = x + g1 * mixer_out(u_fl, y_l, xs_l, z_l, d_skip[layer], ssm_norm_w[layer],
                               w_fourier[layer], w_out[layer])
        if not last:
            ctx = ctx + cg1 * mixer_out(u_fc, y_c, xs_c, z_c, d_skip[layer], ssm_norm_w[layer],
                                        w_fourier[layer], w_out[layer])

        x = x + g2 * ec_moe(modulate(rms_norm(x, norm2_w[layer]), sh2, sc2),
                            w_router[layer], w_gate[layer], w_up[layer], w_down[layer])
        if not last:
            ctx = ctx + cg2 * ec_moe(modulate(rms_norm(ctx, norm2_w[layer]), csh2, csc2),
                                     w_router[layer], w_gate[layer], w_up[layer], w_down[layer])
    return rms_norm(x, final_norm_w)
```

```python
import functools

import jax
import jax.numpy as jnp
from jax import lax
from jax.experimental import pallas as pl
from jax.experimental.pallas import tpu as pltpu

D_MODEL = 1024
GRID_W = 64
F_GROUP_DIM = 64
F_DIM = D_MODEL // 2
F_GROUPS = F_DIM // F_GROUP_DIM
SSM_HEAD_DIM = 64
D_SSM = 3 * D_MODEL // 2
SSM_HEADS = D_SSM // SSM_HEAD_DIM
SSM_GROUPS = 4
D_STATE = 128
CONV_K = 5
CHUNK = 128
D_MIX = F_DIM + D_SSM
CONV_DIM = D_SSM + 2 * SSM_GROUPS * D_STATE
D_IN_PROJ = F_DIM + D_SSM + CONV_DIM + 2 * SSM_HEADS
N_EXPERTS = 16
CAPACITY_FACTOR = 2
D_EXPERT = 2048
N_MOD = 6
EPS = 1e-6

LANE = 128
DT_PAD = LANE
VMEM_LIMIT = 56 * 1024 * 1024

F32 = jnp.float32
BF16 = jnp.bfloat16


def _norm_mod(x, nw, shift, scale):
    ms = jnp.mean(x * x, axis=-1, keepdims=True)
    y = x * lax.rsqrt(ms + EPS) * nw
    return y * (1.0 + scale) + shift


def _inproj_kernel(x_ref, sh_ref, sc_ref, nw_ref, w_ref, *out_refs, splits):
    h = _norm_mod(x_ref[0], nw_ref[...], sh_ref[0], sc_ref[0]).astype(BF16)
    for (start, size), o_ref in zip(splits, out_refs):
        o_ref[0] = jnp.dot(h, w_ref[:, start:start + size],
                           preferred_element_type=F32).astype(o_ref.dtype)


def _inproj(x, shift, scale, nw, w, splits, dtypes, tm):
    bsz, L, D = x.shape
    n_mod = shift.shape[0]
    mod_map = (lambda b, i: (b, 0, 0)) if n_mod > 1 else (lambda b, i: (0, 0, 0))
    return pl.pallas_call(
        functools.partial(_inproj_kernel, splits=splits),
        grid=(bsz, L // tm),
        in_specs=[
            pl.BlockSpec((1, tm, D), lambda b, i: (b, i, 0)),
            pl.BlockSpec((1, 1, D), mod_map),
            pl.BlockSpec((1, 1, D), mod_map),
            pl.BlockSpec((1, D), lambda b, i: (0, 0)),
            pl.BlockSpec(w.shape, lambda b, i: (0, 0)),
        ],
        out_specs=[pl.BlockSpec((1, tm, n), lambda b, i: (b, i, 0)) for _, n in splits],
        out_shape=[jax.ShapeDtypeStruct((bsz, L, n), dt) for (_, n), dt in zip(splits, dtypes)],
        compiler_params=pltpu.CompilerParams(
            dimension_semantics=("parallel", "parallel"),
            vmem_limit_bytes=VMEM_LIMIT),
        name="inproj",
    )(x, shift, scale, nw, w)


def _outproj_kernel(f_ref, y_ref, x_ref, g1_ref, sh_ref, sc_ref, nw_ref, wo_ref, wr_ref,
                    x1_ref, h2_ref, p_ref):
    m = jnp.dot(f_ref[0], wo_ref[:F_DIM, :], preferred_element_type=F32)
    m = m + jnp.dot(y_ref[0], wo_ref[F_DIM:, :], preferred_element_type=F32)
    x1 = x_ref[0] + g1_ref[0] * m
    x1_ref[0] = x1
    h2 = _norm_mod(x1, nw_ref[...], sh_ref[0], sc_ref[0])
    h2_ref[0] = h2.astype(BF16)
    logits = jnp.dot(h2, wr_ref[...], preferred_element_type=F32)
    lane = lax.broadcasted_iota(jnp.int32, logits.shape, 1)
    logits = jnp.where(lane < N_EXPERTS, logits, -jnp.inf)
    e = jnp.exp(logits - jnp.max(logits, axis=-1, keepdims=True))
    p_ref[0] = e / jnp.sum(e, axis=-1, keepdims=True)


def _outproj(four, yssm, x, g1, sh2, sc2, nw, wo, wr, tm):
    bsz, L, D = x.shape
    row = lambda n: pl.BlockSpec((1, tm, n), lambda b, i: (b, i, 0))
    mod = pl.BlockSpec((1, 1, D), lambda b, i: (b, 0, 0))
    return pl.pallas_call(
        _outproj_kernel,
        grid=(bsz, L // tm),
        in_specs=[row(F_DIM), row(D_SSM), row(D), mod, mod, mod,
                  pl.BlockSpec((1, D), lambda b, i: (0, 0)),
                  pl.BlockSpec(wo.shape, lambda b, i: (0, 0)),
                  pl.BlockSpec(wr.shape, lambda b, i: (0, 0))],
        out_specs=[row(D), row(D), row(LANE)],
        out_shape=[jax.ShapeDtypeStruct((bsz, L, D), F32),
                   jax.ShapeDtypeStruct((bsz, L, D), BF16),
                   jax.ShapeDtypeStruct((bsz, L, LANE), F32)],
        compiler_params=pltpu.CompilerParams(
            dimension_semantics=("parallel", "parallel"),
            vmem_limit_bytes=VMEM_LIMIT),
        name="outproj",
    )(four, yssm, x, g1, sh2, sc2, nw, wo, wr)


def _ffn_kernel(x_ref, v_ref, wg_ref, wu_ref, wd_ref, o_ref, *, f_chunk):
    x = x_ref[0, 0]
    acc = jnp.zeros(o_ref.shape[2:], F32)
    for f in range(0, D_EXPERT, f_chunk):
        g = jnp.dot(x, wg_ref[0, :, f:f + f_chunk], preferred_element_type=F32)
        u = jnp.dot(x, wu_ref[0, :, f:f + f_chunk], preferred_element_type=F32)
        a = (g * jax.nn.sigmoid(g) * u).astype(BF16)
        acc = acc + jnp.dot(a, wd_ref[0, f:f + f_chunk, :], preferred_element_type=F32)
    o_ref[0, 0] = acc * v_ref[0, 0]


def _expert_ffn(xg, vals, wg, wu, wd):
    bsz, n_e, cap, D = xg.shape
    return pl.pallas_call(
        functools.partial(_ffn_kernel, f_chunk=512),
        grid=(n_e, bsz),
        in_specs=[
            pl.BlockSpec((1, 1, cap, D), lambda e, b: (b, e, 0, 0)),
            pl.BlockSpec((1, 1, cap, 1), lambda e, b: (b, e, 0, 0)),
            pl.BlockSpec((1, D, D_EXPERT), lambda e, b: (e, 0, 0)),
            pl.BlockSpec((1, D, D_EXPERT), lambda e, b: (e, 0, 0)),
            pl.BlockSpec((1, D_EXPERT, D), lambda e, b: (e, 0, 0)),
        ],
        out_specs=pl.BlockSpec((1, 1, cap, D), lambda e, b: (b, e, 0, 0)),
        out_shape=jax.ShapeDtypeStruct((bsz, n_e, cap, D), F32),
        compiler_params=pltpu.CompilerParams(
            dimension_semantics=("parallel", "parallel"),
            vmem_limit_bytes=VMEM_LIMIT),
        name="expert_ffn",
    )(xg, vals, wg, wu, wd)


def _final_kernel(x_ref, m_ref, g_ref, nw_ref, o_ref):
    x = x_ref[0] + g_ref[0] * m_ref[0]
    ms = jnp.mean(x * x, axis=-1, keepdims=True)
    o_ref[0] = x * lax.rsqrt(ms + EPS) * nw_ref[...]


def _final(x1, moe, g2, nw, tm):
    bsz, L, D = x1.shape
    row = pl.BlockSpec((1, tm, D), lambda b, i: (b, i, 0))
    return pl.pallas_call(
        _final_kernel,
        grid=(bsz, L // tm),
        in_specs=[row, row, pl.BlockSpec((1, 1, D), lambda b, i: (b, 0, 0)),
                  pl.BlockSpec((1, D), lambda b, i: (0, 0))],
        out_specs=row,
        out_shape=jax.ShapeDtypeStruct((bsz, L, D), F32),
        compiler_params=pltpu.CompilerParams(
            dimension_semantics=("parallel", "parallel"),
            vmem_limit_bytes=VMEM_LIMIT),
        name="final_norm",
    )(x1, moe, g2, nw)


def _dwconv(u, w, b):
    out = lax.conv_general_dilated(
        u, w[:, None, :].astype(u.dtype), window_strides=(1,),
        padding=[(CONV_K // 2, CONV_K // 2)],
        dimension_numbers=('NWC', 'WIO', 'NWC'),
        feature_group_count=u.shape[-1])
    return out + b


def _split_mixer(xbc, dt_raw, conv_w, conv_b, dt_bias):
    bsz, L, _ = xbc.shape
    xbc = jax.nn.silu(_dwconv(xbc.astype(F32), conv_w, conv_b))
    xs, bm, cm = jnp.split(xbc, [D_SSM, D_SSM + SSM_GROUPS * D_STATE], axis=-1)
    xs = xs.reshape(bsz, L, SSM_HEADS, SSM_HEAD_DIM)
    bm = bm.reshape(bsz, L, SSM_GROUPS, D_STATE)
    cm = cm.reshape(bsz, L, SSM_GROUPS, D_STATE)
    dt = jax.nn.softplus(dt_raw[..., :2 * SSM_HEADS].reshape(bsz, L, 2, SSM_HEADS) + dt_bias)
    return xs, bm, cm, dt[:, :, 0], dt[:, :, 1]


def _ssd_chunked(xs, dt, a, bm, cm, h0, with_output):
    bsz, L, H, P = xs.shape
    G, N = bm.shape[2], bm.shape[3]
    HG = H // G
    nc = L // CHUNK
    bm = bm.reshape(bsz, nc, CHUNK, G, N)
    cm = cm.reshape(bsz, nc, CHUNK, G, N)
    xdt = (xs * dt[..., None]).reshape(bsz, nc, CHUNK, G, HG, P)
    a_cs = jnp.cumsum((dt * a).reshape(bsz, nc, CHUNK, G, HG), axis=2)
    to_end = jnp.exp(a_cs[:, :, -1:] - a_cs)
    states = jnp.einsum('bclgn,bclgh,bclghp->bcghpn', bm, to_end, xdt)
    chunk_decay = jnp.exp(a_cs[:, :, -1])

    def step(h, inp):
        st, dec = inp
        return dec[..., None, None] * h + st, h

    h_last, h_prev = lax.scan(step, h0.reshape(bsz, G, HG, P, N),
                              (jnp.moveaxis(states, 1, 0), jnp.moveaxis(chunk_decay, 1, 0)))
    h_last = h_last.reshape(bsz, H, P, N)
    if not with_output:
        return None, h_last
    h_prev = jnp.moveaxis(h_prev, 0, 1)
    lower = jnp.tril(jnp.ones((CHUNK, CHUNK), dtype=bool))
    seg = a_cs[:, :, :, None] - a_cs[:, :, None, :]
    decay = jnp.exp(jnp.where(lower[:, :, None, None], seg, -jnp.inf))
    cb = jnp.einsum('bclgn,bcsgn->bclsg', cm, bm)
    y_diag = jnp.einsum('bclsg,bclsgh,bcsghp->bclghp', cb, decay, xdt)
    y_off = jnp.einsum('bclgn,bcghpn,bclgh->bclghp', cm, h_prev, jnp.exp(a_cs))
    return (y_diag + y_off).reshape(bsz, L, H, P), h_last


def _ssd_bidir(xs, bm, cm, dt_f, dt_b, a_log, h0_f, h0_b, with_output):
    a = -jnp.exp(a_log)
    flip = lambda t: jnp.flip(t, axis=1)
    y_f, h_f = _ssd_chunked(xs, dt_f, a[0], bm, cm, h0_f, with_output)
    y_b, h_b = _ssd_chunked(flip(xs), flip(dt_b), a[1], flip(bm), flip(cm), h0_b, with_output)
    y = y_f + flip(y_b) if with_output else None
    return y, h_f, h_b


def _fourier_mix(u_f, w_fourier):
    bsz, L, _ = u_f.shape
    u = u_f.astype(F32).reshape(bsz, L, F_GROUPS, F_GROUP_DIM)
    spec = jnp.fft.fft2(u, axes=(1, 3), norm='ortho').real
    out = jnp.einsum('blgi,gij->blgj', spec, w_fourier)
    return out.reshape(bsz, L, F_DIM)


def _gated_norm(y_ssm, xs, z, d_skip, ssm_norm_w):
    bsz, L, _ = z.shape
    y = y_ssm + d_skip[:, None] * xs
    y = y.reshape(bsz, L, D_SSM) * jax.nn.silu(z.astype(F32))
    yg = y.reshape(bsz, L, SSM_GROUPS, D_SSM // SSM_GROUPS)
    yg = yg * lax.rsqrt(jnp.mean(yg * yg, axis=-1, keepdims=True) + EPS)
    return yg.reshape(bsz, L, D_SSM) * ssm_norm_w


def kernel(x, c, ctx, c_ctx, w_ada, b_ada, norm1_w, w_in, conv_w, conv_b, dt_bias, a_log,
           d_skip, ssm_norm_w, w_fourier, w_out, norm2_w, w_router, w_gate, w_up, w_down,
           final_norm_w):
    bsz, L, D = x.shape
    hi = lax.Precision.HIGHEST
    mod_l = (jnp.dot(jax.nn.silu(c), w_ada[0], precision=hi) + b_ada[0]).reshape(bsz, N_MOD, 1, D)
    mod_c = (jnp.dot(jax.nn.silu(c_ctx), w_ada[0], precision=hi) + b_ada[0]).reshape(N_MOD, 1, 1, D)
    sh1, sc1, g1, sh2, sc2, g2 = [mod_l[:, i] for i in range(N_MOD)]
    csh1, csc1 = mod_c[0], mod_c[1]

    w_in_p = jnp.pad(w_in[0], ((0, 0), (0, DT_PAD - 2 * SSM_HEADS))).astype(BF16)
    xbc0 = F_DIM + D_SSM
    dt0 = xbc0 + CONV_DIM
    nw1 = norm1_w[0][None]

    xbc_c, dt_c = _inproj(ctx, csh1, csc1, nw1, w_in_p,
                          ((xbc0, CONV_DIM), (dt0, DT_PAD)), (BF16, F32), tm=256)
    xs_c, bm_c, cm_c, dtf_c, dtb_c = _split_mixer(xbc_c, dt_c, conv_w[0], conv_b[0], dt_bias[0])
    zeros = jnp.zeros((bsz, SSM_HEADS, SSM_HEAD_DIM, D_STATE), F32)
    _, h_f, h_b = _ssd_bidir(xs_c, bm_c, cm_c, dtf_c, dtb_c, a_log[0], zeros, zeros, False)

    u_f, z, xbc, dt_raw = _inproj(
        x, sh1, sc1, nw1, w_in_p,
        ((0, F_DIM), (F_DIM, D_SSM), (xbc0, CONV_DIM), (dt0, DT_PAD)),
        (BF16, BF16, BF16, F32), tm=512)
    xs, bm, cm, dtf, dtb = _split_mixer(xbc, dt_raw, conv_w[0], conv_b[0], dt_bias[0])
    y_l, _, _ = _ssd_bidir(xs, bm, cm, dtf, dtb, a_log[0], h_f, h_b, True)
    y_ssm = _gated_norm(y_l, xs, z, d_skip[0], ssm_norm_w[0]).astype(BF16)
    four = _fourier_mix(u_f, w_fourier[0]).astype(BF16)

    w_r = jnp.pad(w_router[0], ((0, 0), (0, LANE - N_EXPERTS)))
    x1, h2, probs = _outproj(four, y_ssm, x, g1, sh2, sc2, norm2_w[0][None],
                             w_out[0].astype(BF16), w_r, tm=512)

    cap = CAPACITY_FACTOR * L // N_EXPERTS
    vals, idx = lax.top_k(jnp.swapaxes(probs[..., :N_EXPERTS], 1, 2), cap)
    bidx = jnp.arange(bsz)[:, None, None]
    xg = h2[bidx, idx]
    y = _expert_ffn(xg, vals[..., None], w_gate[0].astype(BF16), w_up[0].astype(BF16),
                    w_down[0].astype(BF16))
    moe = jnp.zeros((bsz, L, D), F32).at[bidx, idx].add(y)
    return _final(x1, moe, g2, final_norm_w[None], tm=512)
```

```python
import functools

import jax
import jax.numpy as jnp
from jax import lax
from jax.experimental import pallas as pl
from jax.experimental.pallas import tpu as pltpu

D_MODEL = 1024
GRID_W = 64
F_GROUP_DIM = 64
F_DIM = D_MODEL // 2
F_GROUPS = F_DIM // F_GROUP_DIM
SSM_HEAD_DIM = 64
D_SSM = 3 * D_MODEL // 2
SSM_HEADS = D_SSM // SSM_HEAD_DIM
SSM_GROUPS = 4
D_STATE = 128
CONV_K = 5
CHUNK = 128
D_MIX = F_DIM + D_SSM
CONV_DIM = D_SSM + 2 * SSM_GROUPS * D_STATE
D_IN_PROJ = F_DIM + D_SSM + CONV_DIM + 2 * SSM_HEADS
N_EXPERTS = 16
CAPACITY_FACTOR = 2
D_EXPERT = 2048
N_MOD = 6
EPS = 1e-6

LANE = 128
DT_PAD = LANE
VMEM_LIMIT = 56 * 1024 * 1024

F32 = jnp.float32
BF16 = jnp.bfloat16


def _norm_mod(x, nw, shift, scale):
    ms = jnp.mean(x * x, axis=-1, keepdims=True)
    y = x * lax.rsqrt(ms + EPS) * nw
    return y * (1.0 + scale) + shift


HALO = 8
XBC0 = F_DIM + D_SSM
DT0 = XBC0 + CONV_DIM
CONV_COLS = 512


def _softplus(v):
    return jnp.maximum(v, 0.0) + jnp.log1p(jnp.exp(-jnp.abs(v)))


def _inproj_kernel(x_ref, xp_ref, xn_ref, sh_ref, sc_ref, nw_ref, w_ref, cw_ref, cb_ref, dtb_ref,
                   *refs, emit_fz):
    if emit_fz:
        uf_ref, z_ref, xs_ref, bc_ref, dt_ref, s_ref = refs
    else:
        xs_ref, bc_ref, dt_ref, s_ref = refs
    i = pl.program_id(1)
    tm = x_ref.shape[1]
    nw, sh, sc = nw_ref[...], sh_ref[0], sc_ref[0]
    h = _norm_mod(x_ref[0], nw, sh, sc).astype(BF16)
    halo = jnp.concatenate([xp_ref[0], xn_ref[0]], axis=0)
    hh = _norm_mod(halo, nw, sh, sc).astype(BF16)
    if emit_fz:
        uf_ref[0] = jnp.dot(h, w_ref[:, 0:F_DIM], preferred_element_type=F32).astype(BF16)
        z_ref[0] = jnp.dot(h, w_ref[:, F_DIM:XBC0], preferred_element_type=F32).astype(BF16)
    dt_raw = jnp.dot(h, w_ref[:, DT0:DT0 + DT_PAD], preferred_element_type=F32)
    dt_ref[0] = _softplus(dt_raw + dtb_ref[...])
    s_ref[HALO:HALO + tm, :] = jnp.dot(h, w_ref[:, XBC0:DT0], preferred_element_type=F32)
    hal = jnp.dot(hh, w_ref[:, XBC0:DT0], preferred_element_type=F32)
    s_ref[0:HALO, :] = jnp.where(i > 0, hal[:HALO], 0.0)
    s_ref[HALO + tm:, :] = jnp.where(i < pl.num_programs(1) - 1, hal[HALO:], 0.0)
    for c0 in range(0, CONV_DIM, CONV_COLS):
        acc = jnp.broadcast_to(cb_ref[:, c0:c0 + CONV_COLS], (tm, CONV_COLS))
        for k in range(CONV_K):
            acc = acc + cw_ref[k:k + 1, c0:c0 + CONV_COLS] * s_ref[
                pl.ds(HALO - CONV_K // 2 + k, tm), c0:c0 + CONV_COLS]
        act = (acc * jax.nn.sigmoid(acc)).astype(BF16)
        if c0 < D_SSM:
            xs_ref[0, :, c0:c0 + CONV_COLS] = act
        else:
            bc_ref[0, :, c0 - D_SSM:c0 - D_SSM + CONV_COLS] = act


def _inproj(x, shift, scale, nw, w, conv_w, conv_b, dt_bias, emit_fz, tm):
    bsz, L, D = x.shape
    n_mod = shift.shape[0]
    mod_map = (lambda b, i: (b, 0, 0)) if n_mod > 1 else (lambda b, i: (0, 0, 0))
    const = lambda b, i: (0, 0)
    row = lambda n: pl.BlockSpec((1, tm, n), lambda b, i: (b, i, 0))
    hb = tm // HALO
    widths = ([F_DIM, D_SSM] if emit_fz else []) + [D_SSM, 2 * SSM_GROUPS * D_STATE, DT_PAD]
    dtypes = ([BF16, BF16] if emit_fz else []) + [BF16, BF16, F32]
    return pl.pallas_call(
        functools.partial(_inproj_kernel, emit_fz=emit_fz),
        grid=(bsz, L // tm),
        in_specs=[
            row(D),
            pl.BlockSpec((1, HALO, D), lambda b, i: (b, jnp.maximum(i * hb - 1, 0), 0)),
            pl.BlockSpec((1, HALO, D), lambda b, i: (b, jnp.minimum((i + 1) * hb, L // HALO - 1), 0)),
            pl.BlockSpec((1, 1, D), mod_map),
            pl.BlockSpec((1, 1, D), mod_map),
            pl.BlockSpec((1, D), const),
            pl.BlockSpec(w.shape, const),
            pl.BlockSpec(conv_w.shape, const),
            pl.BlockSpec(conv_b.shape, const),
            pl.BlockSpec(dt_bias.shape, const),
        ],
        out_specs=[row(n) for n in widths],
        out_shape=[jax.ShapeDtypeStruct((bsz, L, n), dt) for n, dt in zip(widths, dtypes)],
        scratch_shapes=[pltpu.VMEM((tm + 2 * HALO, CONV_DIM), F32)],
        compiler_params=pltpu.CompilerParams(
            dimension_semantics=("parallel", "arbitrary"),
            vmem_limit_bytes=VMEM_LIMIT),
        name="inproj",
    )(x, x, x, shift, scale, nw, w, conv_w, conv_b, dt_bias)


HEADS_PER_GROUP = SSM_HEADS // SSM_GROUPS
GROUP_W = HEADS_PER_GROUP * SSM_HEAD_DIM
PAIRS_PER_GROUP = GROUP_W // LANE


def _ssd_chunk(ldx, ldbc, dt, a_row, st_ref, reverse, emit):
    off = SSM_HEADS if reverse else 0
    end = 0 if reverse else CHUNK - 1
    r = lax.broadcasted_iota(jnp.int32, (CHUNK, CHUNK), 0)
    s = lax.broadcasted_iota(jnp.int32, (CHUNK, CHUNK), 1)
    keep = (r <= s) if reverse else (r >= s)
    left = lax.broadcasted_iota(jnp.int32, (CHUNK, LANE), 1) < SSM_HEAD_DIM
    left_bf = left.astype(BF16)
    right_bf = 1 - left_bf

    def expand(v, j):
        return jnp.where(left[:v.shape[0]], v[:, j:j + 1], v[:, j + 1:j + 2])

    dA = dt * a_row
    tri = keep.astype(BF16)
    p1 = dA.astype(BF16)
    r1 = dA - p1.astype(F32)
    p2 = r1.astype(BF16)
    p3 = (r1 - p2.astype(F32)).astype(BF16)
    acs = (jnp.dot(tri, p1, preferred_element_type=F32)
           + jnp.dot(tri, p2, preferred_element_type=F32)
           + jnp.dot(tri, p3, preferred_element_type=F32))
    acs_end = acs[end:end + 1, :]
    w_state = jnp.exp(acs_end - acs) * dt
    dec_row = jnp.exp(acs_end)
    if emit is not None:
        acs_t = acs.T
        dt_t = dt.T
        e_in = jnp.exp(acs)

    for g in range(SSM_GROUPS):
        bg = ldbc(D_STATE * g, D_STATE)
        cg = ldbc(SSM_GROUPS * D_STATE + D_STATE * g, D_STATE)
        hst = st_ref[g]
        if emit is not None:
            cb = lax.dot_general(cg, bg, (((1,), (1,)), ((), ())), preferred_element_type=F32)
            y_off = jnp.dot(cg, hst.astype(BF16), preferred_element_type=F32)
        xw, dec, ys = [], [], []
        for p in range(PAIRS_PER_GROUP):
            j = off + HEADS_PER_GROUP * g + 2 * p
            xp = ldx(GROUP_W * g + LANE * p, LANE)
            xw.append((xp.astype(F32) * expand(w_state, j)).astype(BF16))
            dec.append(expand(dec_row, j))
            if emit is not None:
                ms = []
                for jj in (j, j + 1):
                    seg = acs[:, jj:jj + 1] - acs_t[jj:jj + 1, :]
                    m = cb * jnp.exp(jnp.where(keep, seg, -jnp.inf)) * dt_t[jj:jj + 1, :]
                    ms.append(m.astype(BF16))
                lhs = jnp.concatenate(ms, axis=1)
                rhs = jnp.concatenate([xp * left_bf, xp * right_bf], axis=0)
                y_d = jnp.dot(lhs, rhs, preferred_element_type=F32)
                ys.append(y_d + y_off[:, LANE * p:LANE * (p + 1)] * expand(e_in, j))
        upd = lax.dot_general(bg, jnp.concatenate(xw, axis=1), (((0,), (0,)), ((), ())),
                              preferred_element_type=F32)
        st_ref[g] = hst * jnp.concatenate(dec, axis=1) + upd
        if emit is not None:
            emit(g, jnp.concatenate(ys, axis=1))


def _ssd_kernel(*refs, reverse, final):
    if final:
        (xs_ref, bc_ref, dt_ref, xsc_ref, bcc_ref, dtc_ref, ar_ref,
         yf_ref, z_ref, dsk_ref, gw_ref, o_ref, st_ref) = refs
    else:
        xs_ref, bc_ref, dt_ref, xsc_ref, bcc_ref, dtc_ref, ar_ref, o_ref, st_ref = refs
    n_chunks = xs_ref.shape[1] // CHUNK
    n_ctx = xsc_ref.shape[1] // CHUNK
    a_row = ar_ref[...]

    @pl.when(pl.program_id(1) == 0)
    def _():
        st_ref[...] = jnp.zeros(st_ref.shape, F32)
        for c in (range(n_ctx - 1, -1, -1) if reverse else range(n_ctx)):
            rows = pl.ds(c * CHUNK, CHUNK)
            _ssd_chunk(lambda c0, w: xsc_ref[0, rows, c0:c0 + w],
                       lambda c0, w: bcc_ref[0, rows, c0:c0 + w],
                       dtc_ref[0, rows, :], a_row, st_ref, reverse, None)

    def body(k, carry):
        c = (n_chunks - 1 - k) if reverse else k
        rows = pl.ds(pl.multiple_of(c * CHUNK, CHUNK), CHUNK)

        def emit(g, y):
            cols = slice(GROUP_W * g, GROUP_W * (g + 1))
            if final:
                y = y + yf_ref[0, rows, cols].astype(F32)
                y = y + dsk_ref[:, cols] * xs_ref[0, rows, cols].astype(F32)
                z = z_ref[0, rows, cols].astype(F32)
                y = y * (z * jax.nn.sigmoid(z))
                y = y * lax.rsqrt(jnp.mean(y * y, axis=-1, keepdims=True) + EPS) * gw_ref[:, cols]
            o_ref[0, rows, cols] = y.astype(BF16)

        _ssd_chunk(lambda c0, w: xs_ref[0, rows, c0:c0 + w],
                   lambda c0, w: bc_ref[0, rows, c0:c0 + w],
                   dt_ref[0, rows, :], a_row, st_ref, reverse, emit)
        return carry

    lax.fori_loop(0, n_chunks, body, 0)


def _ssd(xs, bc, dt, xs_c, bc_c, dt_c, a_row, reverse, extra, rows_per_step):
    bsz, L, _ = xs.shape
    n_blk = L // rows_per_step
    blk = (lambda b, i: (b, n_blk - 1 - i, 0)) if reverse else (lambda b, i: (b, i, 0))
    row = lambda n: pl.BlockSpec((1, rows_per_step, n), blk)
    ctx = lambda a: pl.BlockSpec((1,) + a.shape[1:], lambda b, i: (b, 0, 0))
    const = lambda a: pl.BlockSpec(a.shape, lambda b, i: (0, 0))
    in_specs = [row(D_SSM), row(bc.shape[2]), row(DT_PAD), ctx(xs_c), ctx(bc_c), ctx(dt_c),
                const(a_row)]
    args = [xs, bc, dt, xs_c, bc_c, dt_c, a_row]
    if extra is not None:
        y_fwd, z, dsk, gw = extra
        in_specs += [row(D_SSM), row(D_SSM), const(dsk), const(gw)]
        args += [y_fwd, z, dsk, gw]
    return pl.pallas_call(
        functools.partial(_ssd_kernel, reverse=reverse, final=extra is not None),
        grid=(bsz, n_blk),
        in_specs=in_specs,
        out_specs=row(D_SSM),
        out_shape=jax.ShapeDtypeStruct((bsz, L, D_SSM), BF16),
        scratch_shapes=[pltpu.VMEM((SSM_GROUPS, D_STATE, GROUP_W), F32)],
        compiler_params=pltpu.CompilerParams(
            dimension_semantics=("parallel", "arbitrary"),
            vmem_limit_bytes=VMEM_LIMIT),
        name="ssd_bwd" if reverse else "ssd_fwd",
    )(*args)


def _outproj_kernel(f_ref, y_ref, x_ref, g1_ref, sh_ref, sc_ref, nw_ref, wo_ref, wr_ref,
                    x1_ref, h2_ref, p_ref):
    m = jnp.dot(f_ref[0], wo_ref[:F_DIM, :], preferred_element_type=F32)
    m = m + jnp.dot(y_ref[0], wo_ref[F_DIM:, :], preferred_element_type=F32)
    x1 = x_ref[0] + g1_ref[0] * m
    x1_ref[0] = x1
    h2 = _norm_mod(x1, nw_ref[...], sh_ref[0], sc_ref[0])
    h2_ref[0] = h2.astype(BF16)
    logits = jnp.dot(h2, wr_ref[...], preferred_element_type=F32)
    lane = lax.broadcasted_iota(jnp.int32, logits.shape, 1)
    logits = jnp.where(lane < N_EXPERTS, logits, -jnp.inf)
    e = jnp.exp(logits - jnp.max(logits, axis=-1, keepdims=True))
    p_ref[0] = e / jnp.sum(e, axis=-1, keepdims=True)


def _outproj(four, yssm, x, g1, sh2, sc2, nw, wo, wr, tm):
    bsz, L, D = x.shape
    row = lambda n: pl.BlockSpec((1, tm, n), lambda b, i: (b, i, 0))
    mod = pl.BlockSpec((1, 1, D), lambda b, i: (b, 0, 0))
    return pl.pallas_call(
        _outproj_kernel,
        grid=(bsz, L // tm),
        in_specs=[row(F_DIM), row(D_SSM), row(D), mod, mod, mod,
                  pl.BlockSpec((1, D), lambda b, i: (0, 0)),
                  pl.BlockSpec(wo.shape, lambda b, i: (0, 0)),
                  pl.BlockSpec(wr.shape, lambda b, i: (0, 0))],
        out_specs=[row(D), row(D), row(LANE)],
        out_shape=[jax.ShapeDtypeStruct((bsz, L, D), F32),
                   jax.ShapeDtypeStruct((bsz, L, D), BF16),
                   jax.ShapeDtypeStruct((bsz, L, LANE), F32)],
        compiler_params=pltpu.CompilerParams(
            dimension_semantics=("parallel", "parallel"),
            vmem_limit_bytes=VMEM_LIMIT),
        name="outproj",
    )(four, yssm, x, g1, sh2, sc2, nw, wo, wr)


def _ffn_kernel(x_ref, v_ref, wg_ref, wu_ref, wd_ref, o_ref, *, f_chunk):
    x = x_ref[0, 0]
    acc = jnp.zeros(o_ref.shape[2:], F32)
    for f in range(0, D_EXPERT, f_chunk):
        g = jnp.dot(x, wg_ref[0, :, f:f + f_chunk], preferred_element_type=F32)
        u = jnp.dot(x, wu_ref[0, :, f:f + f_chunk], preferred_element_type=F32)
        a = (g * jax.nn.sigmoid(g) * u).astype(BF16)
        acc = acc + jnp.dot(a, wd_ref[0, f:f + f_chunk, :], preferred_element_type=F32)
    o_ref[0, 0] = acc * v_ref[0, 0]


def _expert_ffn(xg, vals, wg, wu, wd):
    bsz, n_e, cap, D = xg.shape
    return pl.pallas_call(
        functools.partial(_ffn_kernel, f_chunk=512),
        grid=(n_e, bsz),
        in_specs=[
            pl.BlockSpec((1, 1, cap, D), lambda e, b: (b, e, 0, 0)),
            pl.BlockSpec((1, 1, cap, 1), lambda e, b: (b, e, 0, 0)),
            pl.BlockSpec((1, D, D_EXPERT), lambda e, b: (e, 0, 0)),
            pl.BlockSpec((1, D, D_EXPERT), lambda e, b: (e, 0, 0)),
            pl.BlockSpec((1, D_EXPERT, D), lambda e, b: (e, 0, 0)),
        ],
        out_specs=pl.BlockSpec((1, 1, cap, D), lambda e, b: (b, e, 0, 0)),
        out_shape=jax.ShapeDtypeStruct((bsz, n_e, cap, D), F32),
        compiler_params=pltpu.CompilerParams(
            dimension_semantics=("parallel", "parallel"),
            vmem_limit_bytes=VMEM_LIMIT),
        name="expert_ffn",
    )(xg, vals, wg, wu, wd)


def _final_kernel(x_ref, m_ref, g_ref, nw_ref, o_ref):
    x = x_ref[0] + g_ref[0] * m_ref[0]
    ms = jnp.mean(x * x, axis=-1, keepdims=True)
    o_ref[0] = x * lax.rsqrt(ms + EPS) * nw_ref[...]


def _final(x1, moe, g2, nw, tm):
    bsz, L, D = x1.shape
    row = pl.BlockSpec((1, tm, D), lambda b, i: (b, i, 0))
    return pl.pallas_call(
        _final_kernel,
        grid=(bsz, L // tm),
        in_specs=[row, row, pl.BlockSpec((1, 1, D), lambda b, i: (b, 0, 0)),
                  pl.BlockSpec((1, D), lambda b, i: (0, 0))],
        out_specs=row,
        out_shape=jax.ShapeDtypeStruct((bsz, L, D), F32),
        compiler_params=pltpu.CompilerParams(
            dimension_semantics=("parallel", "parallel"),
            vmem_limit_bytes=VMEM_LIMIT),
        name="final_norm",
    )(x1, moe, g2, nw)


def _fourier_mix(u_f, w_fourier):
    bsz, L, _ = u_f.shape
    u = u_f.astype(F32).reshape(bsz, L, F_GROUPS, F_GROUP_DIM)
    spec = jnp.fft.fft2(u, axes=(1, 3), norm='ortho').real
    out = jnp.einsum('blgi,gij->blgj', spec, w_fourier)
    return out.reshape(bsz, L, F_DIM)


def kernel(x, c, ctx, c_ctx, w_ada, b_ada, norm1_w, w_in, conv_w, conv_b, dt_bias, a_log,
           d_skip, ssm_norm_w, w_fourier, w_out, norm2_w, w_router, w_gate, w_up, w_down,
           final_norm_w):
    bsz, L, D = x.shape
    hi = lax.Precision.HIGHEST
    mod_l = (jnp.dot(jax.nn.silu(c), w_ada[0], precision=hi) + b_ada[0]).reshape(bsz, N_MOD, 1, D)
    mod_c = (jnp.dot(jax.nn.silu(c_ctx), w_ada[0], precision=hi) + b_ada[0]).reshape(N_MOD, 1, 1, D)
    sh1, sc1, g1, sh2, sc2, g2 = [mod_l[:, i] for i in range(N_MOD)]
    csh1, csc1 = mod_c[0], mod_c[1]

    dt_pad = DT_PAD - 2 * SSM_HEADS
    w_in_p = jnp.pad(w_in[0], ((0, 0), (0, dt_pad))).astype(BF16)
    dtb_row = jnp.pad(dt_bias[0].reshape(1, 2 * SSM_HEADS), ((0, 0), (0, dt_pad)))
    a_row = jnp.pad(-jnp.exp(a_log[0]).reshape(1, 2 * SSM_HEADS), ((0, 0), (0, dt_pad)))
    nw1 = norm1_w[0][None]
    conv_b_row = conv_b[0][None]

    xs_c, bc_c, dt_c = _inproj(ctx, csh1, csc1, nw1, w_in_p, conv_w[0], conv_b_row, dtb_row,
                               emit_fz=False, tm=ctx.shape[1])
    u_f, z, xs, bc, dt = _inproj(x, sh1, sc1, nw1, w_in_p, conv_w[0], conv_b_row, dtb_row,
                                 emit_fz=True, tm=512)
    y_fwd = _ssd(xs, bc, dt, xs_c, bc_c, dt_c, a_row, False, None, rows_per_step=1024)
    dsk_row = jnp.repeat(d_skip[0], SSM_HEAD_DIM)[None]
    y_ssm = _ssd(xs, bc, dt, xs_c, bc_c, dt_c, a_row, True,
                 (y_fwd, z, dsk_row, ssm_norm_w[0][None]), rows_per_step=1024)
    four = _fourier_mix(u_f, w_fourier[0]).astype(BF16)

    w_r = jnp.pad(w_router[0], ((0, 0), (0, LANE - N_EXPERTS)))
    x1, h2, probs = _outproj(four, y_ssm, x, g1, sh2, sc2, norm2_w[0][None],
                             w_out[0].astype(BF16), w_r, tm=512)

    cap = CAPACITY_FACTOR * L // N_EXPERTS
    vals, idx = lax.top_k(jnp.swapaxes(probs[..., :N_EXPERTS], 1, 2), cap)
    bidx = jnp.arange(bsz)[:, None, None]
    xg = h2[bidx, idx]
    y = _expert_ffn(xg, vals[..., None], w_gate[0].astype(BF16), w_up[0].astype(BF16),
                    w_down[0].astype(BF16))
    moe = jnp.zeros((bsz, L, D), F32).at[bidx, idx].add(y)
    return _final(x1, moe, g2, final_norm_w[None], tm=512)
```

```python
import functools

import jax
import jax.numpy as jnp
from jax import lax
from jax.experimental import pallas as pl
from jax.experimental.pallas import tpu as pltpu

D_MODEL = 1024
GRID_W = 64
F_GROUP_DIM = 64
F_DIM = D_MODEL // 2
F_GROUPS = F_DIM // F_GROUP_DIM
SSM_HEAD_DIM = 64
D_SSM = 3 * D_MODEL // 2
SSM_HEADS = D_SSM // SSM_HEAD_DIM
SSM_GROUPS = 4
D_STATE = 128
CONV_K = 5
CHUNK = 128
D_MIX = F_DIM + D_SSM
CONV_DIM = D_SSM + 2 * SSM_GROUPS * D_STATE
D_IN_PROJ = F_DIM + D_SSM + CONV_DIM + 2 * SSM_HEADS
N_EXPERTS = 16
CAPACITY_FACTOR = 2
D_EXPERT = 2048
N_MOD = 6
EPS = 1e-6

LANE = 128
DT_PAD = LANE
VMEM_LIMIT = 56 * 1024 * 1024

F32 = jnp.float32
BF16 = jnp.bfloat16


def _norm_mod(x, nw, shift, scale):
    ms = jnp.mean(x * x, axis=-1, keepdims=True)
    y = x * lax.rsqrt(ms + EPS) * nw
    return y * (1.0 + scale) + shift


HALO = 8
Z0 = 2 * F_DIM
XBC0 = Z0 + D_SSM
DT0 = XBC0 + CONV_DIM
CONV_COLS = 512


def _softplus(v):
    return jnp.maximum(v, 0.0) + jnp.log1p(jnp.exp(-jnp.abs(v)))


def _inproj_kernel(x_ref, xp_ref, xn_ref, sh_ref, sc_ref, nw_ref, w_ref, cw_ref, cb_ref, dtb_ref,
                   *refs, emit_fz):
    if emit_fz:
        p_ref, q_ref, z_ref, xs_ref, bc_ref, dt_ref, s_ref = refs
    else:
        xs_ref, bc_ref, dt_ref, s_ref = refs
    i = pl.program_id(1)
    tm = x_ref.shape[1]
    nw, sh, sc = nw_ref[...], sh_ref[0], sc_ref[0]
    h = _norm_mod(x_ref[0], nw, sh, sc).astype(BF16)
    halo = jnp.concatenate([xp_ref[0], xn_ref[0]], axis=0)
    hh = _norm_mod(halo, nw, sh, sc).astype(BF16)
    if emit_fz:
        p_ref[0] = jnp.dot(h, w_ref[:, 0:F_DIM], preferred_element_type=F32).astype(BF16)
        q_ref[0] = jnp.dot(h, w_ref[:, F_DIM:Z0], preferred_element_type=F32).astype(BF16)
        z_ref[0] = jnp.dot(h, w_ref[:, Z0:XBC0], preferred_element_type=F32).astype(BF16)
    dt_raw = jnp.dot(h, w_ref[:, DT0:DT0 + DT_PAD], preferred_element_type=F32)
    dt_ref[0] = _softplus(dt_raw + dtb_ref[...])
    s_ref[HALO:HALO + tm, :] = jnp.dot(h, w_ref[:, XBC0:DT0], preferred_element_type=F32)
    hal = jnp.dot(hh, w_ref[:, XBC0:DT0], preferred_element_type=F32)
    s_ref[0:HALO, :] = jnp.where(i > 0, hal[:HALO], 0.0)
    s_ref[HALO + tm:, :] = jnp.where(i < pl.num_programs(1) - 1, hal[HALO:], 0.0)
    for c0 in range(0, CONV_DIM, CONV_COLS):
        acc = jnp.broadcast_to(cb_ref[:, c0:c0 + CONV_COLS], (tm, CONV_COLS))
        for k in range(CONV_K):
            acc = acc + cw_ref[k:k + 1, c0:c0 + CONV_COLS] * s_ref[
                pl.ds(HALO - CONV_K // 2 + k, tm), c0:c0 + CONV_COLS]
        act = (acc * jax.nn.sigmoid(acc)).astype(BF16)
        if c0 < D_SSM:
            xs_ref[0, :, c0:c0 + CONV_COLS] = act
        else:
            bc_ref[0, :, c0 - D_SSM:c0 - D_SSM + CONV_COLS] = act


def _inproj(x, shift, scale, nw, w, conv_w, conv_b, dt_bias, emit_fz, tm):
    bsz, L, D = x.shape
    n_mod = shift.shape[0]
    mod_map = (lambda b, i: (b, 0, 0)) if n_mod > 1 else (lambda b, i: (0, 0, 0))
    const = lambda b, i: (0, 0)
    row = lambda n: pl.BlockSpec((1, tm, n), lambda b, i: (b, i, 0))
    hb = tm // HALO
    widths = ([F_DIM, F_DIM, D_SSM] if emit_fz else []) + [D_SSM, 2 * SSM_GROUPS * D_STATE, DT_PAD]
    dtypes = ([BF16, BF16, BF16] if emit_fz else []) + [BF16, BF16, F32]
    return pl.pallas_call(
        functools.partial(_inproj_kernel, emit_fz=emit_fz),
        grid=(bsz, L // tm),
        in_specs=[
            row(D),
            pl.BlockSpec((1, HALO, D), lambda b, i: (b, jnp.maximum(i * hb - 1, 0), 0)),
            pl.BlockSpec((1, HALO, D), lambda b, i: (b, jnp.minimum((i + 1) * hb, L // HALO - 1), 0)),
            pl.BlockSpec((1, 1, D), mod_map),
            pl.BlockSpec((1, 1, D), mod_map),
            pl.BlockSpec((1, D), const),
            pl.BlockSpec(w.shape, const),
            pl.BlockSpec(conv_w.shape, const),
            pl.BlockSpec(conv_b.shape, const),
            pl.BlockSpec(dt_bias.shape, const),
        ],
        out_specs=[row(n) for n in widths],
        out_shape=[jax.ShapeDtypeStruct((bsz, L, n), dt) for n, dt in zip(widths, dtypes)],
        scratch_shapes=[pltpu.VMEM((tm + 2 * HALO, CONV_DIM), F32)],
        compiler_params=pltpu.CompilerParams(
            dimension_semantics=("parallel", "arbitrary"),
            vmem_limit_bytes=VMEM_LIMIT),
        name="inproj",
    )(x, x, x, shift, scale, nw, w, conv_w, conv_b, dt_bias)


HEADS_PER_GROUP = SSM_HEADS // SSM_GROUPS
GROUP_W = HEADS_PER_GROUP * SSM_HEAD_DIM
PAIRS_PER_GROUP = GROUP_W // LANE


def _ssd_chunk(ldx, ldbc, dt, a_row, st_ref, reverse, emit):
    off = SSM_HEADS if reverse else 0
    end = 0 if reverse else CHUNK - 1
    r = lax.broadcasted_iota(jnp.int32, (CHUNK, CHUNK), 0)
    s = lax.broadcasted_iota(jnp.int32, (CHUNK, CHUNK), 1)
    keep = (r <= s) if reverse else (r >= s)
    left = lax.broadcasted_iota(jnp.int32, (CHUNK, LANE), 1) < SSM_HEAD_DIM
    left_bf = left.astype(BF16)
    right_bf = 1 - left_bf

    def expand(v, j):
        return jnp.where(left[:v.shape[0]], v[:, j:j + 1], v[:, j + 1:j + 2])

    dA = dt * a_row
    tri = keep.astype(BF16)
    p1 = dA.astype(BF16)
    r1 = dA - p1.astype(F32)
    p2 = r1.astype(BF16)
    p3 = (r1 - p2.astype(F32)).astype(BF16)
    acs = (jnp.dot(tri, p1, preferred_element_type=F32)
           + jnp.dot(tri, p2, preferred_element_type=F32)
           + jnp.dot(tri, p3, preferred_element_type=F32))
    acs_end = acs[end:end + 1, :]
    w_state = jnp.exp(acs_end - acs) * dt
    dec_row = jnp.exp(acs_end)
    if emit is not None:
        acs_t = acs.T
        dt_t = dt.T
        e_in = jnp.exp(acs)

    for g in range(SSM_GROUPS):
        bg = ldbc(D_STATE * g, D_STATE)
        cg = ldbc(SSM_GROUPS * D_STATE + D_STATE * g, D_STATE)
        hst = st_ref[g]
        if emit is not None:
            cb = lax.dot_general(cg, bg, (((1,), (1,)), ((), ())), preferred_element_type=F32)
            y_off = jnp.dot(cg, hst.astype(BF16), preferred_element_type=F32)
        xw, dec, ys = [], [], []
        for p in range(PAIRS_PER_GROUP):
            j = off + HEADS_PER_GROUP * g + 2 * p
            xp = ldx(GROUP_W * g + LANE * p, LANE)
            xw.append((xp.astype(F32) * expand(w_state, j)).astype(BF16))
            dec.append(expand(dec_row, j))
            if emit is not None:
                ms = []
                for jj in (j, j + 1):
                    seg = acs[:, jj:jj + 1] - acs_t[jj:jj + 1, :]
                    m = cb * jnp.exp(jnp.where(keep, seg, -jnp.inf)) * dt_t[jj:jj + 1, :]
                    ms.append(m.astype(BF16))
                lhs = jnp.concatenate(ms, axis=1)
                rhs = jnp.concatenate([xp * left_bf, xp * right_bf], axis=0)
                y_d = jnp.dot(lhs, rhs, preferred_element_type=F32)
                ys.append(y_d + y_off[:, LANE * p:LANE * (p + 1)] * expand(e_in, j))
        upd = lax.dot_general(bg, jnp.concatenate(xw, axis=1), (((0,), (0,)), ((), ())),
                              preferred_element_type=F32)
        st_ref[g] = hst * jnp.concatenate(dec, axis=1) + upd
        if emit is not None:
            emit(g, jnp.concatenate(ys, axis=1))


def _ssd_kernel(*refs, reverse, final):
    if final:
        (xs_ref, bc_ref, dt_ref, xsc_ref, bcc_ref, dtc_ref, ar_ref,
         yf_ref, z_ref, dsk_ref, gw_ref, o_ref, st_ref) = refs
    else:
        xs_ref, bc_ref, dt_ref, xsc_ref, bcc_ref, dtc_ref, ar_ref, o_ref, st_ref = refs
    n_chunks = xs_ref.shape[1] // CHUNK
    n_ctx = xsc_ref.shape[1] // CHUNK
    a_row = ar_ref[...]

    @pl.when(pl.program_id(1) == 0)
    def _():
        st_ref[...] = jnp.zeros(st_ref.shape, F32)
        for c in (range(n_ctx - 1, -1, -1) if reverse else range(n_ctx)):
            rows = pl.ds(c * CHUNK, CHUNK)
            _ssd_chunk(lambda c0, w: xsc_ref[0, rows, c0:c0 + w],
                       lambda c0, w: bcc_ref[0, rows, c0:c0 + w],
                       dtc_ref[0, rows, :], a_row, st_ref, reverse, None)

    def body(k, carry):
        c = (n_chunks - 1 - k) if reverse else k
        rows = pl.ds(pl.multiple_of(c * CHUNK, CHUNK), CHUNK)

        def emit(g, y):
            cols = slice(GROUP_W * g, GROUP_W * (g + 1))
            if final:
                y = y + yf_ref[0, rows, cols].astype(F32)
                y = y + dsk_ref[:, cols] * xs_ref[0, rows, cols].astype(F32)
                z = z_ref[0, rows, cols].astype(F32)
                y = y * (z * jax.nn.sigmoid(z))
                y = y * lax.rsqrt(jnp.mean(y * y, axis=-1, keepdims=True) + EPS) * gw_ref[:, cols]
            o_ref[0, rows, cols] = y.astype(BF16)

        _ssd_chunk(lambda c0, w: xs_ref[0, rows, c0:c0 + w],
                   lambda c0, w: bc_ref[0, rows, c0:c0 + w],
                   dt_ref[0, rows, :], a_row, st_ref, reverse, emit)
        return carry

    lax.fori_loop(0, n_chunks, body, 0)


def _ssd(xs, bc, dt, xs_c, bc_c, dt_c, a_row, reverse, extra, rows_per_step):
    bsz, L, _ = xs.shape
    n_blk = L // rows_per_step
    blk = (lambda b, i: (b, n_blk - 1 - i, 0)) if reverse else (lambda b, i: (b, i, 0))
    row = lambda n: pl.BlockSpec((1, rows_per_step, n), blk)
    ctx = lambda a: pl.BlockSpec((1,) + a.shape[1:], lambda b, i: (b, 0, 0))
    const = lambda a: pl.BlockSpec(a.shape, lambda b, i: (0, 0))
    in_specs = [row(D_SSM), row(bc.shape[2]), row(DT_PAD), ctx(xs_c), ctx(bc_c), ctx(dt_c),
                const(a_row)]
    args = [xs, bc, dt, xs_c, bc_c, dt_c, a_row]
    if extra is not None:
        y_fwd, z, dsk, gw = extra
        in_specs += [row(D_SSM), row(D_SSM), const(dsk), const(gw)]
        args += [y_fwd, z, dsk, gw]
    return pl.pallas_call(
        functools.partial(_ssd_kernel, reverse=reverse, final=extra is not None),
        grid=(bsz, n_blk),
        in_specs=in_specs,
        out_specs=row(D_SSM),
        out_shape=jax.ShapeDtypeStruct((bsz, L, D_SSM), BF16),
        scratch_shapes=[pltpu.VMEM((SSM_GROUPS, D_STATE, GROUP_W), F32)],
        compiler_params=pltpu.CompilerParams(
            dimension_semantics=("parallel", "arbitrary"),
            vmem_limit_bytes=VMEM_LIMIT),
        name="ssd_bwd" if reverse else "ssd_fwd",
    )(*args)


def _outproj_kernel(f_ref, y_ref, x_ref, g1_ref, sh_ref, sc_ref, nw_ref, wo_ref, wr_ref,
                    x1_ref, h2_ref, p_ref):
    m = jnp.dot(f_ref[0], wo_ref[:F_DIM, :], preferred_element_type=F32)
    m = m + jnp.dot(y_ref[0], wo_ref[F_DIM:, :], preferred_element_type=F32)
    x1 = x_ref[0] + g1_ref[0] * m
    x1_ref[0] = x1
    h2 = _norm_mod(x1, nw_ref[...], sh_ref[0], sc_ref[0])
    h2_ref[0] = h2.astype(BF16)
    logits = jnp.dot(h2, wr_ref[...], preferred_element_type=F32)
    lane = lax.broadcasted_iota(jnp.int32, logits.shape, 1)
    logits = jnp.where(lane < N_EXPERTS, logits, -jnp.inf)
    e = jnp.exp(logits - jnp.max(logits, axis=-1, keepdims=True))
    p_ref[0] = e / jnp.sum(e, axis=-1, keepdims=True)


def _outproj(four, yssm, x, g1, sh2, sc2, nw, wo, wr, tm):
    bsz, L, D = x.shape
    row = lambda n: pl.BlockSpec((1, tm, n), lambda b, i: (b, i, 0))
    mod = pl.BlockSpec((1, 1, D), lambda b, i: (b, 0, 0))
    return pl.pallas_call(
        _outproj_kernel,
        grid=(bsz, L // tm),
        in_specs=[row(F_DIM), row(D_SSM), row(D), mod, mod, mod,
                  pl.BlockSpec((1, D), lambda b, i: (0, 0)),
                  pl.BlockSpec(wo.shape, lambda b, i: (0, 0)),
                  pl.BlockSpec(wr.shape, lambda b, i: (0, 0))],
        out_specs=[row(D), row(D), row(LANE)],
        out_shape=[jax.ShapeDtypeStruct((bsz, L, D), F32),
                   jax.ShapeDtypeStruct((bsz, L, D), BF16),
                   jax.ShapeDtypeStruct((bsz, L, LANE), F32)],
        compiler_params=pltpu.CompilerParams(
            dimension_semantics=("parallel", "parallel"),
            vmem_limit_bytes=VMEM_LIMIT),
        name="outproj",
    )(four, yssm, x, g1, sh2, sc2, nw, wo, wr)


def _ffn_kernel(x_ref, v_ref, wg_ref, wu_ref, wd_ref, o_ref, *, f_chunk):
    x = x_ref[0, 0]
    acc = jnp.zeros(o_ref.shape[2:], F32)
    for f in range(0, D_EXPERT, f_chunk):
        g = jnp.dot(x, wg_ref[0, :, f:f + f_chunk], preferred_element_type=F32)
        u = jnp.dot(x, wu_ref[0, :, f:f + f_chunk], preferred_element_type=F32)
        a = (g * jax.nn.sigmoid(g) * u).astype(BF16)
        acc = acc + jnp.dot(a, wd_ref[0, f:f + f_chunk, :], preferred_element_type=F32)
    o_ref[0, 0] = (acc * v_ref[0, 0]).astype(o_ref.dtype)


def _expert_ffn(xg, vals, wg, wu, wd):
    bsz, n_e, cap, D = xg.shape
    return pl.pallas_call(
        functools.partial(_ffn_kernel, f_chunk=512),
        grid=(n_e, bsz),
        in_specs=[
            pl.BlockSpec((1, 1, cap, D), lambda e, b: (b, e, 0, 0)),
            pl.BlockSpec((1, 1, cap, 1), lambda e, b: (b, e, 0, 0)),
            pl.BlockSpec((1, D, D_EXPERT), lambda e, b: (e, 0, 0)),
            pl.BlockSpec((1, D, D_EXPERT), lambda e, b: (e, 0, 0)),
            pl.BlockSpec((1, D_EXPERT, D), lambda e, b: (e, 0, 0)),
        ],
        out_specs=pl.BlockSpec((1, 1, cap, D), lambda e, b: (b, e, 0, 0)),
        out_shape=jax.ShapeDtypeStruct((bsz, n_e, cap, D), BF16),
        compiler_params=pltpu.CompilerParams(
            dimension_semantics=("parallel", "parallel"),
            vmem_limit_bytes=VMEM_LIMIT),
        name="expert_ffn",
    )(xg, vals, wg, wu, wd)


def _combine_kernel(idx_ref, y_ref, x1_ref, g_ref, nw_ref, o_ref, acc_ref):
    e = pl.program_id(2)
    tb = x1_ref.shape[1]
    cap = y_ref.shape[2]

    @pl.when(e == 0)
    def _():
        acc_ref[...] = jnp.zeros(acc_ref.shape, F32)

    tok = pl.program_id(1) * tb + lax.broadcasted_iota(jnp.int32, (tb, cap), 0)
    onehot = (tok == idx_ref[0, 0]).astype(BF16)
    acc_ref[...] += jnp.dot(onehot, y_ref[0, 0], preferred_element_type=F32)

    @pl.when(e == pl.num_programs(2) - 1)
    def _():
        x = x1_ref[0] + g_ref[0] * acc_ref[...]
        ms = jnp.mean(x * x, axis=-1, keepdims=True)
        o_ref[0] = x * lax.rsqrt(ms + EPS) * nw_ref[...]


def _combine(idx, y, x1, g2, nw, tb):
    bsz, L, D = x1.shape
    n_e, cap = y.shape[1], y.shape[2]
    row = pl.BlockSpec((1, tb, D), lambda b, t, e: (b, t, 0))
    return pl.pallas_call(
        _combine_kernel,
        grid=(bsz, L // tb, n_e),
        in_specs=[pl.BlockSpec((1, 1, 1, cap), lambda b, t, e: (b, e, 0, 0)),
                  pl.BlockSpec((1, 1, cap, D), lambda b, t, e: (b, e, 0, 0)),
                  row,
                  pl.BlockSpec((1, 1, D), lambda b, t, e: (b, 0, 0)),
                  pl.BlockSpec((1, D), lambda b, t, e: (0, 0))],
        out_specs=row,
        out_shape=jax.ShapeDtypeStruct((bsz, L, D), F32),
        scratch_shapes=[pltpu.VMEM((tb, D), F32)],
        compiler_params=pltpu.CompilerParams(
            dimension_semantics=("parallel", "parallel", "arbitrary"),
            vmem_limit_bytes=VMEM_LIMIT),
        name="combine",
    )(idx, y, x1, g2, nw)


DFT_LANES = 8 * F_DIM
DFT_ROWS = 8


def _dft1_kernel(p_ref, q_ref, mh_ref, ml_ref, tc_ref, ts_ref, o_ref):
    w = GRID_W
    v = jnp.concatenate([p_ref[0], q_ref[0]], axis=0)
    y = (jnp.dot(mh_ref[...], v, preferred_element_type=F32)
         + jnp.dot(ml_ref[...], v, preferred_element_type=F32))
    yr, yi = y[:w], y[w:]
    tc, ts = tc_ref[...], ts_ref[...]
    o_ref[0, :w] = (yr * tc + yi * ts).astype(BF16)
    o_ref[0, w:] = (yi * tc - yr * ts).astype(BF16)


def _dft2_kernel(yr_ref, yi_ref, mh_ref, ml_ref, o_ref):
    for k in range(DFT_ROWS):
        v = jnp.concatenate([yr_ref[0, k], yi_ref[0, k]], axis=0)
        o = (jnp.dot(mh_ref[...], v, preferred_element_type=F32)
             + jnp.dot(ml_ref[...], v, preferred_element_type=F32))
        o_ref[0, :, k * F_DIM:(k + 1) * F_DIM] = o.astype(BF16)


def _split_bf16(m):
    hi = m.astype(BF16)
    return hi, (m - hi.astype(F32)).astype(BF16)


def _fourier_positions(p, q):
    bsz, L, _ = p.shape
    w = GRID_W
    wide = w * F_DIM
    k = jnp.arange(w, dtype=F32)
    ang = (2.0 * jnp.pi / w) * jnp.outer(k, k)
    c, s = jnp.cos(ang), jnp.sin(ang)
    m1h, m1l = _split_bf16(jnp.block([[c, -s], [-s, -c]]))
    m2h, m2l = _split_bf16(jnp.concatenate([c, s], axis=1))
    tw = (2.0 * jnp.pi / L) * jnp.outer(k, k)
    tc = jnp.repeat(jnp.cos(tw), F_DIM, axis=1)
    ts = jnp.repeat(jnp.sin(tw), F_DIM, axis=1)
    const = lambda a: pl.BlockSpec(a.shape, lambda b, i: (0, 0))
    y1 = pl.pallas_call(
        _dft1_kernel,
        grid=(bsz, wide // DFT_LANES),
        in_specs=[pl.BlockSpec((1, w, DFT_LANES), lambda b, i: (b, 0, i)),
                  pl.BlockSpec((1, w, DFT_LANES), lambda b, i: (b, 0, i)),
                  const(m1h), const(m1l),
                  pl.BlockSpec((w, DFT_LANES), lambda b, i: (0, i)),
                  pl.BlockSpec((w, DFT_LANES), lambda b, i: (0, i))],
        out_specs=pl.BlockSpec((1, 2 * w, DFT_LANES), lambda b, i: (b, 0, i)),
        out_shape=jax.ShapeDtypeStruct((bsz, 2 * w, wide), BF16),
        compiler_params=pltpu.CompilerParams(
            dimension_semantics=("parallel", "parallel"), vmem_limit_bytes=VMEM_LIMIT),
        name="dft_stage1",
    )(p.reshape(bsz, w, wide), q.reshape(bsz, w, wide), m1h, m1l, tc, ts)
    y1 = y1.reshape(bsz, 2 * w, w, F_DIM)
    n_blk = w // DFT_ROWS
    out = pl.pallas_call(
        _dft2_kernel,
        grid=(bsz, n_blk),
        in_specs=[pl.BlockSpec((1, DFT_ROWS, w, F_DIM), lambda b, i: (b, i, 0, 0)),
                  pl.BlockSpec((1, DFT_ROWS, w, F_DIM), lambda b, i: (b, n_blk + i, 0, 0)),
                  const(m2h), const(m2l)],
        out_specs=pl.BlockSpec((1, w, DFT_ROWS * F_DIM), lambda b, i: (b, 0, i)),
        out_shape=jax.ShapeDtypeStruct((bsz, w, wide), BF16),
        compiler_params=pltpu.CompilerParams(
            dimension_semantics=("parallel", "parallel"), vmem_limit_bytes=VMEM_LIMIT),
        name="dft_stage2",
    )(y1, y1, m2h, m2l)
    return out.reshape(bsz, L, F_DIM)


def _fourier_fold(w_uf, w_fourier):
    n = F_GROUP_DIM
    k = jnp.arange(n, dtype=F32)
    ang = (2.0 * jnp.pi / n) * jnp.outer(k, k)
    hi = lax.Precision.HIGHEST
    scale = 1.0 / jnp.sqrt(jnp.float32(GRID_W * GRID_W * n))
    a = jnp.einsum('cm,gmj->gcj', jnp.cos(ang), w_fourier, precision=hi) * scale
    b = jnp.einsum('cm,gmj->gcj', jnp.sin(ang), w_fourier, precision=hi) * scale
    wg = w_uf.reshape(-1, F_GROUPS, n)
    wp = jnp.einsum('dgc,gcj->dgj', wg, a, precision=hi).reshape(-1, F_DIM)
    wq = jnp.einsum('dgc,gcj->dgj', wg, b, precision=hi).reshape(-1, F_DIM)
    return wp, wq


def kernel(x, c, ctx, c_ctx, w_ada, b_ada, norm1_w, w_in, conv_w, conv_b, dt_bias, a_log,
           d_skip, ssm_norm_w, w_fourier, w_out, norm2_w, w_router, w_gate, w_up, w_down,
           final_norm_w):
    bsz, L, D = x.shape
    hi = lax.Precision.HIGHEST
    mod_l = (jnp.dot(jax.nn.silu(c), w_ada[0], precision=hi) + b_ada[0]).reshape(bsz, N_MOD, 1, D)
    mod_c = (jnp.dot(jax.nn.silu(c_ctx), w_ada[0], precision=hi) + b_ada[0]).reshape(N_MOD, 1, 1, D)
    sh1, sc1, g1, sh2, sc2, g2 = [mod_l[:, i] for i in range(N_MOD)]
    csh1, csc1 = mod_c[0], mod_c[1]

    dt_pad = DT_PAD - 2 * SSM_HEADS
    assert L == GRID_W * GRID_W
    w_p, w_q = _fourier_fold(w_in[0][:, :F_DIM], w_fourier[0])
    w_in_p = jnp.concatenate(
        [w_p, w_q, w_in[0][:, F_DIM:], jnp.zeros((D, dt_pad), F32)], axis=1).astype(BF16)
    dtb_row = jnp.pad(dt_bias[0].reshape(1, 2 * SSM_HEADS), ((0, 0), (0, dt_pad)))
    a_row = jnp.pad(-jnp.exp(a_log[0]).reshape(1, 2 * SSM_HEADS), ((0, 0), (0, dt_pad)))
    nw1 = norm1_w[0][None]
    conv_b_row = conv_b[0][None]

    xs_c, bc_c, dt_c = _inproj(ctx, csh1, csc1, nw1, w_in_p, conv_w[0], conv_b_row, dtb_row,
                               emit_fz=False, tm=ctx.shape[1])
    p, q, z, xs, bc, dt = _inproj(x, sh1, sc1, nw1, w_in_p, conv_w[0], conv_b_row, dtb_row,
                                  emit_fz=True, tm=512)
    y_fwd = _ssd(xs, bc, dt, xs_c, bc_c, dt_c, a_row, False, None, rows_per_step=1024)
    dsk_row = jnp.repeat(d_skip[0], SSM_HEAD_DIM)[None]
    y_ssm = _ssd(xs, bc, dt, xs_c, bc_c, dt_c, a_row, True,
                 (y_fwd, z, dsk_row, ssm_norm_w[0][None]), rows_per_step=1024)
    four = _fourier_positions(p, q)

    w_r = jnp.pad(w_router[0], ((0, 0), (0, LANE - N_EXPERTS)))
    x1, h2, probs = _outproj(four, y_ssm, x, g1, sh2, sc2, norm2_w[0][None],
                             w_out[0].astype(BF16), w_r, tm=512)

    cap = CAPACITY_FACTOR * L // N_EXPERTS
    vals, idx = lax.top_k(jnp.swapaxes(probs[..., :N_EXPERTS], 1, 2), cap)
    bidx = jnp.arange(bsz)[:, None, None]
    xg = h2[bidx, idx]
    y = _expert_ffn(xg, vals[..., None], w_gate[0].astype(BF16), w_up[0].astype(BF16),
                    w_down[0].astype(BF16))
    return _combine(idx[:, :, None, :], y, x1, g2, final_norm_w[None], tb=2048)
```

```python
import functools

import jax
import jax.numpy as jnp
from jax import lax
from jax.experimental import pallas as pl
from jax.experimental.pallas import tpu as pltpu

D_MODEL = 1024
GRID_W = 64
F_GROUP_DIM = 64
F_DIM = D_MODEL // 2
F_GROUPS = F_DIM // F_GROUP_DIM
SSM_HEAD_DIM = 64
D_SSM = 3 * D_MODEL // 2
SSM_HEADS = D_SSM // SSM_HEAD_DIM
SSM_GROUPS = 4
D_STATE = 128
CONV_K = 5
CHUNK = 128
D_MIX = F_DIM + D_SSM
CONV_DIM = D_SSM + 2 * SSM_GROUPS * D_STATE
D_IN_PROJ = F_DIM + D_SSM + CONV_DIM + 2 * SSM_HEADS
N_EXPERTS = 16
CAPACITY_FACTOR = 2
D_EXPERT = 2048
N_MOD = 6
EPS = 1e-6

LANE = 128
DT_PAD = LANE
VMEM_LIMIT = 56 * 1024 * 1024

F32 = jnp.float32
BF16 = jnp.bfloat16


def _norm_mod(x, nw, shift, scale):
    ms = jnp.mean(x * x, axis=-1, keepdims=True)
    y = x * lax.rsqrt(ms + EPS) * nw
    return y * (1.0 + scale) + shift


HALO = 8
Z0 = 2 * F_DIM
XBC0 = Z0 + D_SSM
DT0 = XBC0 + CONV_DIM
CONV_COLS = 512


def _softplus(v):
    return jnp.maximum(v, 0.0) + jnp.log1p(jnp.exp(-jnp.abs(v)))


def _inproj_kernel(x_ref, xp_ref, xn_ref, sh_ref, sc_ref, nw_ref, w_ref, cw_ref, cb_ref, dtb_ref,
                   *refs, emit_fz):
    if emit_fz:
        p_ref, q_ref, z_ref, xs_ref, bc_ref, dt_ref, s_ref = refs
    else:
        xs_ref, bc_ref, dt_ref, s_ref = refs
    i = pl.program_id(1)
    tm = x_ref.shape[1]
    nw, sh, sc = nw_ref[...], sh_ref[0], sc_ref[0]
    h = _norm_mod(x_ref[0], nw, sh, sc).astype(BF16)
    halo = jnp.concatenate([xp_ref[0], xn_ref[0]], axis=0)
    hh = _norm_mod(halo, nw, sh, sc).astype(BF16)
    if emit_fz:
        p_ref[0] = jnp.dot(h, w_ref[:, 0:F_DIM], preferred_element_type=F32).astype(BF16)
        q_ref[0] = jnp.dot(h, w_ref[:, F_DIM:Z0], preferred_element_type=F32).astype(BF16)
        z_ref[0] = jnp.dot(h, w_ref[:, Z0:XBC0], preferred_element_type=F32).astype(BF16)
    dt_raw = jnp.dot(h, w_ref[:, DT0:DT0 + DT_PAD], preferred_element_type=F32)
    dt_ref[0] = _softplus(dt_raw + dtb_ref[...])
    for ci, c0 in enumerate(range(0, CONV_DIM, CONV_COLS)):
        wc = w_ref[:, XBC0 + c0:XBC0 + c0 + CONV_COLS]
        s_ref[ci, HALO:HALO + tm, :] = jnp.dot(h, wc, preferred_element_type=F32)
        hal = jnp.dot(hh, wc, preferred_element_type=F32)
        s_ref[ci, 0:HALO, :] = jnp.where(i > 0, hal[:HALO], 0.0)
        s_ref[ci, HALO + tm:, :] = jnp.where(i < pl.num_programs(1) - 1, hal[HALO:], 0.0)
        acc = jnp.broadcast_to(cb_ref[:, c0:c0 + CONV_COLS], (tm, CONV_COLS))
        for k in range(CONV_K):
            acc = acc + cw_ref[k:k + 1, c0:c0 + CONV_COLS] * s_ref[
                ci, pl.ds(HALO - CONV_K // 2 + k, tm), :]
        act = (acc * jax.nn.sigmoid(acc)).astype(BF16)
        if c0 < D_SSM:
            xs_ref[0, :, c0:c0 + CONV_COLS] = act
        else:
            bc_ref[0, :, c0 - D_SSM:c0 - D_SSM + CONV_COLS] = act


def _inproj(x, shift, scale, nw, w, conv_w, conv_b, dt_bias, emit_fz, tm):
    bsz, L, D = x.shape
    n_mod = shift.shape[0]
    mod_map = (lambda b, i: (b, 0, 0)) if n_mod > 1 else (lambda b, i: (0, 0, 0))
    const = lambda b, i: (0, 0)
    row = lambda n: pl.BlockSpec((1, tm, n), lambda b, i: (b, i, 0))
    hb = tm // HALO
    widths = ([F_DIM, F_DIM, D_SSM] if emit_fz else []) + [D_SSM, 2 * SSM_GROUPS * D_STATE, DT_PAD]
    dtypes = ([BF16, BF16, BF16] if emit_fz else []) + [BF16, BF16, F32]
    return pl.pallas_call(
        functools.partial(_inproj_kernel, emit_fz=emit_fz),
        grid=(bsz, L // tm),
        in_specs=[
            row(D),
            pl.BlockSpec((1, HALO, D), lambda b, i: (b, jnp.maximum(i * hb - 1, 0), 0)),
            pl.BlockSpec((1, HALO, D), lambda b, i: (b, jnp.minimum((i + 1) * hb, L // HALO - 1), 0)),
            pl.BlockSpec((1, 1, D), mod_map),
            pl.BlockSpec((1, 1, D), mod_map),
            pl.BlockSpec((1, D), const),
            pl.BlockSpec(w.shape, const),
            pl.BlockSpec(conv_w.shape, const),
            pl.BlockSpec(conv_b.shape, const),
            pl.BlockSpec(dt_bias.shape, const),
        ],
        out_specs=[row(n) for n in widths],
        out_shape=[jax.ShapeDtypeStruct((bsz, L, n), dt) for n, dt in zip(widths, dtypes)],
        scratch_shapes=[pltpu.VMEM((CONV_DIM // CONV_COLS, tm + 2 * HALO, CONV_COLS), F32)],
        compiler_params=pltpu.CompilerParams(
            dimension_semantics=("parallel", "arbitrary"),
            vmem_limit_bytes=VMEM_LIMIT),
        name="inproj",
    )(x, x, x, shift, scale, nw, w, conv_w, conv_b, dt_bias)


HEADS_PER_GROUP = SSM_HEADS // SSM_GROUPS
GROUP_W = HEADS_PER_GROUP * SSM_HEAD_DIM
PAIRS_PER_GROUP = GROUP_W // LANE


def _ssd_chunk(ldx, ldbc, dt, a_row, st_ref, rt_ref, reverse, emit):
    off = SSM_HEADS if reverse else 0
    end = 0 if reverse else CHUNK - 1
    r = lax.broadcasted_iota(jnp.int32, (CHUNK, CHUNK), 0)
    s = lax.broadcasted_iota(jnp.int32, (CHUNK, CHUNK), 1)
    keep = (r <= s) if reverse else (r >= s)
    left = lax.broadcasted_iota(jnp.int32, (CHUNK, LANE), 1) < SSM_HEAD_DIM
    left_bf = left.astype(BF16)
    right_bf = 1 - left_bf

    def expand(v, j):
        if v.shape[0] == 1:
            return jnp.where(left[:1], v[:, j:j + 1], v[:, j + 1:j + 2])
        return jnp.take_along_axis(v, jnp.where(left, j, j + 1), axis=1,
                                   mode="promise_in_bounds")

    dA = dt * a_row
    tri = keep.astype(BF16)
    p1 = dA.astype(BF16)
    r1 = dA - p1.astype(F32)
    p2 = r1.astype(BF16)
    p3 = (r1 - p2.astype(F32)).astype(BF16)
    acs = (jnp.dot(tri, p1, preferred_element_type=F32)
           + jnp.dot(tri, p2, preferred_element_type=F32)
           + jnp.dot(tri, p3, preferred_element_type=F32))
    acs_end = acs[end:end + 1, :]
    w_state = jnp.exp(acs_end - acs) * dt
    dec_row = jnp.exp(acs_end)
    if emit is not None:
        rt_ref[...] = (jnp.log(dt) - acs).T
        e_in = jnp.exp(acs)

    for g in range(SSM_GROUPS):
        bg = ldbc(D_STATE * g, D_STATE)
        cg = ldbc(SSM_GROUPS * D_STATE + D_STATE * g, D_STATE)
        hst = st_ref[g]
        if emit is not None:
            cb = lax.dot_general(cg, bg, (((1,), (1,)), ((), ())), preferred_element_type=F32)
            y_off = jnp.dot(cg, hst.astype(BF16), preferred_element_type=F32)
        xw, dec, ys = [], [], []
        for p in range(PAIRS_PER_GROUP):
            j = off + HEADS_PER_GROUP * g + 2 * p
            xp = ldx(GROUP_W * g + LANE * p, LANE)
            xw.append((xp.astype(F32) * expand(w_state, j)).astype(BF16))
            dec.append(expand(dec_row, j))
            if emit is not None:
                ms = []
                for jj in (j, j + 1):
                    seg = acs[:, jj:jj + 1] + rt_ref[jj:jj + 1, :]
                    ms.append((cb * jnp.exp(jnp.where(keep, seg, -jnp.inf))).astype(BF16))
                lhs = jnp.concatenate(ms, axis=1)
                rhs = jnp.concatenate([xp * left_bf, xp * right_bf], axis=0)
                y_d = jnp.dot(lhs, rhs, preferred_element_type=F32)
                ys.append(y_d + y_off[:, LANE * p:LANE * (p + 1)] * expand(e_in, j))
        upd = lax.dot_general(bg, jnp.concatenate(xw, axis=1), (((0,), (0,)), ((), ())),
                              preferred_element_type=F32)
        st_ref[g] = hst * jnp.concatenate(dec, axis=1) + upd
        if emit is not None:
            emit(g, jnp.concatenate(ys, axis=1))


def _ssd_kernel(*refs, reverse, final):
    if final:
        (xs_ref, bc_ref, dt_ref, xsc_ref, bcc_ref, dtc_ref, ar_ref,
         yf_ref, z_ref, dsk_ref, gw_ref, o_ref, st_ref, rt_ref) = refs
    else:
        xs_ref, bc_ref, dt_ref, xsc_ref, bcc_ref, dtc_ref, ar_ref, o_ref, st_ref, rt_ref = refs
    n_chunks = xs_ref.shape[1] // CHUNK
    n_ctx = xsc_ref.shape[1] // CHUNK
    a_row = ar_ref[...]

    @pl.when(pl.program_id(1) == 0)
    def _():
        st_ref[...] = jnp.zeros(st_ref.shape, F32)
        for c in (range(n_ctx - 1, -1, -1) if reverse else range(n_ctx)):
            rows = pl.ds(c * CHUNK, CHUNK)
            _ssd_chunk(lambda c0, w: xsc_ref[0, rows, c0:c0 + w],
                       lambda c0, w: bcc_ref[0, rows, c0:c0 + w],
                       dtc_ref[0, rows, :], a_row, st_ref, rt_ref, reverse, None)

    def body(k, carry):
        c = (n_chunks - 1 - k) if reverse else k
        rows = pl.ds(pl.multiple_of(c * CHUNK, CHUNK), CHUNK)

        def emit(g, y):
            cols = slice(GROUP_W * g, GROUP_W * (g + 1))
            if final:
                y = y + yf_ref[0, rows, cols].astype(F32)
                y = y + dsk_ref[:, cols] * xs_ref[0, rows, cols].astype(F32)
                z = z_ref[0, rows, cols].astype(F32)
                y = y * (z * jax.nn.sigmoid(z))
                y = y * lax.rsqrt(jnp.mean(y * y, axis=-1, keepdims=True) + EPS) * gw_ref[:, cols]
            o_ref[0, rows, cols] = y.astype(BF16)

        _ssd_chunk(lambda c0, w: xs_ref[0, rows, c0:c0 + w],
                   lambda c0, w: bc_ref[0, rows, c0:c0 + w],
                   dt_ref[0, rows, :], a_row, st_ref, rt_ref, reverse, emit)
        return carry

    lax.fori_loop(0, n_chunks, body, 0)


def _ssd(xs, bc, dt, xs_c, bc_c, dt_c, a_row, reverse, extra, rows_per_step):
    bsz, L, _ = xs.shape
    n_blk = L // rows_per_step
    blk = (lambda b, i: (b, n_blk - 1 - i, 0)) if reverse else (lambda b, i: (b, i, 0))
    row = lambda n: pl.BlockSpec((1, rows_per_step, n), blk)
    ctx = lambda a: pl.BlockSpec((1,) + a.shape[1:], lambda b, i: (b, 0, 0))
    const = lambda a: pl.BlockSpec(a.shape, lambda b, i: (0, 0))
    in_specs = [row(D_SSM), row(bc.shape[2]), row(DT_PAD), ctx(xs_c), ctx(bc_c), ctx(dt_c),
                const(a_row)]
    args = [xs, bc, dt, xs_c, bc_c, dt_c, a_row]
    if extra is not None:
        y_fwd, z, dsk, gw = extra
        in_specs += [row(D_SSM), row(D_SSM), const(dsk), const(gw)]
        args += [y_fwd, z, dsk, gw]
    return pl.pallas_call(
        functools.partial(_ssd_kernel, reverse=reverse, final=extra is not None),
        grid=(bsz, n_blk),
        in_specs=in_specs,
        out_specs=row(D_SSM),
        out_shape=jax.ShapeDtypeStruct((bsz, L, D_SSM), BF16),
        scratch_shapes=[pltpu.VMEM((SSM_GROUPS, D_STATE, GROUP_W), F32),
                        pltpu.VMEM((DT_PAD, CHUNK), F32)],
        compiler_params=pltpu.CompilerParams(
            dimension_semantics=("parallel", "arbitrary"),
            vmem_limit_bytes=VMEM_LIMIT),
        name="ssd_bwd" if reverse else "ssd_fwd",
    )(*args)


def _outproj_kernel(f_ref, y_ref, x_ref, g1_ref, sh_ref, sc_ref, nw_ref, wo_ref, wr_ref,
                    x1_ref, h2_ref, p_ref):
    m = jnp.dot(f_ref[0], wo_ref[:F_DIM, :], preferred_element_type=F32)
    m = m + jnp.dot(y_ref[0], wo_ref[F_DIM:, :], preferred_element_type=F32)
    x1 = x_ref[0] + g1_ref[0] * m
    x1_ref[0] = x1
    h2 = _norm_mod(x1, nw_ref[...], sh_ref[0], sc_ref[0])
    h2_ref[0] = h2.astype(BF16)
    logits = jnp.dot(h2, wr_ref[...], preferred_element_type=F32)
    lane = lax.broadcasted_iota(jnp.int32, logits.shape, 1)
    logits = jnp.where(lane < N_EXPERTS, logits, -jnp.inf)
    e = jnp.exp(logits - jnp.max(logits, axis=-1, keepdims=True))
    p_ref[0] = e / jnp.sum(e, axis=-1, keepdims=True)


def _outproj(four, yssm, x, g1, sh2, sc2, nw, wo, wr, tm):
    bsz, L, D = x.shape
    row = lambda n: pl.BlockSpec((1, tm, n), lambda b, i: (b, i, 0))
    mod = pl.BlockSpec((1, 1, D), lambda b, i: (b, 0, 0))
    return pl.pallas_call(
        _outproj_kernel,
        grid=(bsz, L // tm),
        in_specs=[row(F_DIM), row(D_SSM), row(D), mod, mod, mod,
                  pl.BlockSpec((1, D), lambda b, i: (0, 0)),
                  pl.BlockSpec(wo.shape, lambda b, i: (0, 0)),
                  pl.BlockSpec(wr.shape, lambda b, i: (0, 0))],
        out_specs=[row(D), row(D), row(LANE)],
        out_shape=[jax.ShapeDtypeStruct((bsz, L, D), F32),
                   jax.ShapeDtypeStruct((bsz, L, D), BF16),
                   jax.ShapeDtypeStruct((bsz, L, LANE), F32)],
        compiler_params=pltpu.CompilerParams(
            dimension_semantics=("parallel", "parallel"),
            vmem_limit_bytes=VMEM_LIMIT),
        name="outproj",
    )(four, yssm, x, g1, sh2, sc2, nw, wo, wr)


FFN_SLABS = 2
FFN_SLAB = D_EXPERT // FFN_SLABS
FFN_COLS = 512


def _ffn_kernel(x_ref, v_ref, wg_ref, wu_ref, wd_ref, o_ref, wgb_ref, wub_ref, wdb_ref, acc_ref):
    b, f = pl.program_id(1), pl.program_id(2)

    @pl.when(b == 0)
    def _():
        wgb_ref[f] = wg_ref[0].astype(BF16)
        wub_ref[f] = wu_ref[0].astype(BF16)
        wdb_ref[f] = wd_ref[0].astype(BF16)

    x = x_ref[0, 0]
    part = jnp.zeros(acc_ref.shape, F32)
    for c in range(0, FFN_SLAB, FFN_COLS):
        g = jnp.dot(x, wgb_ref[f, :, c:c + FFN_COLS], preferred_element_type=F32)
        u = jnp.dot(x, wub_ref[f, :, c:c + FFN_COLS], preferred_element_type=F32)
        a = (g * jax.nn.sigmoid(g) * u).astype(BF16)
        part = part + jnp.dot(a, wdb_ref[f, c:c + FFN_COLS, :], preferred_element_type=F32)

    @pl.when(f == 0)
    def _():
        acc_ref[...] = part

    @pl.when(f > 0)
    def _():
        acc_ref[...] += part

    @pl.when(f == FFN_SLABS - 1)
    def _():
        o_ref[0, 0] = (acc_ref[...] * v_ref[0, 0]).astype(o_ref.dtype)


def _expert_ffn(xg, vals, wg, wu, wd):
    bsz, n_e, cap, D = xg.shape
    slab = lambda b, f: jnp.where(b == 0, f, FFN_SLABS - 1)
    return pl.pallas_call(
        _ffn_kernel,
        grid=(n_e, bsz, FFN_SLABS),
        in_specs=[
            pl.BlockSpec((1, 1, cap, D), lambda e, b, f: (b, e, 0, 0)),
            pl.BlockSpec((1, 1, cap, 1), lambda e, b, f: (b, e, 0, 0)),
            pl.BlockSpec((1, D, FFN_SLAB), lambda e, b, f: (e, 0, slab(b, f))),
            pl.BlockSpec((1, D, FFN_SLAB), lambda e, b, f: (e, 0, slab(b, f))),
            pl.BlockSpec((1, FFN_SLAB, D), lambda e, b, f: (e, slab(b, f), 0)),
        ],
        out_specs=pl.BlockSpec((1, 1, cap, D), lambda e, b, f: (b, e, 0, 0)),
        out_shape=jax.ShapeDtypeStruct((bsz, n_e, cap, D), BF16),
        scratch_shapes=[pltpu.VMEM((FFN_SLABS, D, FFN_SLAB), BF16),
                        pltpu.VMEM((FFN_SLABS, D, FFN_SLAB), BF16),
                        pltpu.VMEM((FFN_SLABS, FFN_SLAB, D), BF16),
                        pltpu.VMEM((cap, D), F32)],
        compiler_params=pltpu.CompilerParams(
            dimension_semantics=("arbitrary", "arbitrary", "arbitrary"),
            vmem_limit_bytes=VMEM_LIMIT),
        name="expert_ffn",
    )(xg, vals, wg, wu, wd)


def _combine_kernel(idx_ref, y_ref, x1_ref, g_ref, nw_ref, o_ref, acc_ref):
    e = pl.program_id(2)
    tb = x1_ref.shape[1]
    n_e, cap, d = y_ref.shape[1:]

    @pl.when(e == 0)
    def _():
        acc_ref[...] = jnp.zeros(acc_ref.shape, F32)

    tok = pl.program_id(1) * tb + lax.broadcasted_iota(jnp.int32, (tb, cap), 0)
    onehot = jnp.concatenate([(tok == idx_ref[0, k]).astype(BF16) for k in range(n_e)], axis=1)
    acc_ref[...] += jnp.dot(onehot, y_ref[0].reshape(n_e * cap, d), preferred_element_type=F32)

    @pl.when(e == pl.num_programs(2) - 1)
    def _():
        x = x1_ref[0] + g_ref[0] * acc_ref[...]
        ms = jnp.mean(x * x, axis=-1, keepdims=True)
        o_ref[0] = x * lax.rsqrt(ms + EPS) * nw_ref[...]


def _combine(idx, y, x1, g2, nw, tb, e_blk):
    bsz, L, D = x1.shape
    n_e, cap = y.shape[1], y.shape[2]
    row = pl.BlockSpec((1, tb, D), lambda b, t, e: (b, t, 0))
    return pl.pallas_call(
        _combine_kernel,
        grid=(bsz, L // tb, n_e // e_blk),
        in_specs=[pl.BlockSpec((1, e_blk, 1, cap), lambda b, t, e: (b, e, 0, 0)),
                  pl.BlockSpec((1, e_blk, cap, D), lambda b, t, e: (b, e, 0, 0)),
                  row,
                  pl.BlockSpec((1, 1, D), lambda b, t, e: (b, 0, 0)),
                  pl.BlockSpec((1, D), lambda b, t, e: (0, 0))],
        out_specs=row,
        out_shape=jax.ShapeDtypeStruct((bsz, L, D), F32),
        scratch_shapes=[pltpu.VMEM((tb, D), F32)],
        compiler_params=pltpu.CompilerParams(
            dimension_semantics=("parallel", "parallel", "arbitrary"),
            vmem_limit_bytes=VMEM_LIMIT),
        name="combine",
    )(idx, y, x1, g2, nw)


DFT_LANES = 8 * F_DIM
DFT_ROWS = 8


def _dft1_kernel(p_ref, q_ref, mh_ref, ml_ref, tc_ref, ts_ref, o_ref):
    w = GRID_W
    v = jnp.concatenate([p_ref[0], q_ref[0]], axis=0)
    y = (jnp.dot(mh_ref[...], v, preferred_element_type=F32)
         + jnp.dot(ml_ref[...], v, preferred_element_type=F32))
    yr, yi = y[:w], y[w:]
    tc, ts = tc_ref[...], ts_ref[...]
    o_ref[0, :w] = (yr * tc + yi * ts).astype(BF16)
    o_ref[0, w:] = (yi * tc - yr * ts).astype(BF16)


def _dft2_kernel(yr_ref, yi_ref, mh_ref, ml_ref, o_ref):
    for k in range(DFT_ROWS):
        v = jnp.concatenate([yr_ref[0, k], yi_ref[0, k]], axis=0)
        o = (jnp.dot(mh_ref[...], v, preferred_element_type=F32)
             + jnp.dot(ml_ref[...], v, preferred_element_type=F32))
        o_ref[0, :, k * F_DIM:(k + 1) * F_DIM] = o.astype(BF16)


def _split_bf16(m):
    hi = m.astype(BF16)
    return hi, (m - hi.astype(F32)).astype(BF16)


def _fourier_positions(p, q):
    bsz, L, _ = p.shape
    w = GRID_W
    wide = w * F_DIM
    k = jnp.arange(w, dtype=F32)
    ang = (2.0 * jnp.pi / w) * jnp.outer(k, k)
    c, s = jnp.cos(ang), jnp.sin(ang)
    m1h, m1l = _split_bf16(jnp.block([[c, -s], [-s, -c]]))
    m2h, m2l = _split_bf16(jnp.concatenate([c, s], axis=1))
    tw = (2.0 * jnp.pi / L) * jnp.outer(k, k)
    tc = jnp.repeat(jnp.cos(tw), F_DIM, axis=1)
    ts = jnp.repeat(jnp.sin(tw), F_DIM, axis=1)
    const = lambda a: pl.BlockSpec(a.shape, lambda b, i: (0, 0))
    y1 = pl.pallas_call(
        _dft1_kernel,
        grid=(bsz, wide // DFT_LANES),
        in_specs=[pl.BlockSpec((1, w, DFT_LANES), lambda b, i: (b, 0, i)),
                  pl.BlockSpec((1, w, DFT_LANES), lambda b, i: (b, 0, i)),
                  const(m1h), const(m1l),
                  pl.BlockSpec((w, DFT_LANES), lambda b, i: (0, i)),
                  pl.BlockSpec((w, DFT_LANES), lambda b, i: (0, i))],
        out_specs=pl.BlockSpec((1, 2 * w, DFT_LANES), lambda b, i: (b, 0, i)),
        out_shape=jax.ShapeDtypeStruct((bsz, 2 * w, wide), BF16),
        compiler_params=pltpu.CompilerParams(
            dimension_semantics=("parallel", "parallel"), vmem_limit_bytes=VMEM_LIMIT),
        name="dft_stage1",
    )(p.reshape(bsz, w, wide), q.reshape(bsz, w, wide), m1h, m1l, tc, ts)
    y1 = y1.reshape(bsz, 2 * w, w, F_DIM)
    n_blk = w // DFT_ROWS
    out = pl.pallas_call(
        _dft2_kernel,
        grid=(bsz, n_blk),
        in_specs=[pl.BlockSpec((1, DFT_ROWS, w, F_DIM), lambda b, i: (b, i, 0, 0)),
                  pl.BlockSpec((1, DFT_ROWS, w, F_DIM), lambda b, i: (b, n_blk + i, 0, 0)),
                  const(m2h), const(m2l)],
        out_specs=pl.BlockSpec((1, w, DFT_ROWS * F_DIM), lambda b, i: (b, 0, i)),
        out_shape=jax.ShapeDtypeStruct((bsz, w, wide), BF16),
        compiler_params=pltpu.CompilerParams(
            dimension_semantics=("parallel", "parallel"), vmem_limit_bytes=VMEM_LIMIT),
        name="dft_stage2",
    )(y1, y1, m2h, m2l)
    return out.reshape(bsz, L, F_DIM)


def _fourier_fold(w_uf, w_fourier):
    n = F_GROUP_DIM
    k = jnp.arange(n, dtype=F32)
    ang = (2.0 * jnp.pi / n) * jnp.outer(k, k)
    hi = lax.Precision.HIGHEST
    scale = 1.0 / jnp.sqrt(jnp.float32(GRID_W * GRID_W * n))
    a = jnp.einsum('cm,gmj->gcj', jnp.cos(ang), w_fourier, precision=hi) * scale
    b = jnp.einsum('cm,gmj->gcj', jnp.sin(ang), w_fourier, precision=hi) * scale
    wg = w_uf.reshape(-1, F_GROUPS, n)
    wp = jnp.einsum('dgc,gcj->dgj', wg, a, precision=hi).reshape(-1, F_DIM)
    wq = jnp.einsum('dgc,gcj->dgj', wg, b, precision=hi).reshape(-1, F_DIM)
    return wp, wq


def kernel(x, c, ctx, c_ctx, w_ada, b_ada, norm1_w, w_in, conv_w, conv_b, dt_bias, a_log,
           d_skip, ssm_norm_w, w_fourier, w_out, norm2_w, w_router, w_gate, w_up, w_down,
           final_norm_w):
    bsz, L, D = x.shape
    hi = lax.Precision.HIGHEST
    mod_l = (jnp.dot(jax.nn.silu(c), w_ada[0], precision=hi) + b_ada[0]).reshape(bsz, N_MOD, 1, D)
    mod_c = (jnp.dot(jax.nn.silu(c_ctx), w_ada[0], precision=hi) + b_ada[0]).reshape(N_MOD, 1, 1, D)
    sh1, sc1, g1, sh2, sc2, g2 = [mod_l[:, i] for i in range(N_MOD)]
    csh1, csc1 = mod_c[0], mod_c[1]

    dt_pad = DT_PAD - 2 * SSM_HEADS
    assert L == GRID_W * GRID_W
    w_p, w_q = _fourier_fold(w_in[0][:, :F_DIM], w_fourier[0])
    w_in_p = jnp.concatenate(
        [w_p, w_q, w_in[0][:, F_DIM:], jnp.zeros((D, dt_pad), F32)], axis=1).astype(BF16)
    dtb_row = jnp.pad(dt_bias[0].reshape(1, 2 * SSM_HEADS), ((0, 0), (0, dt_pad)))
    a_row = jnp.pad(-jnp.exp(a_log[0]).reshape(1, 2 * SSM_HEADS), ((0, 0), (0, dt_pad)))
    nw1 = norm1_w[0][None]
    conv_b_row = conv_b[0][None]

    xs_c, bc_c, dt_c = _inproj(ctx, csh1, csc1, nw1, w_in_p, conv_w[0], conv_b_row, dtb_row,
                               emit_fz=False, tm=ctx.shape[1])
    p, q, z, xs, bc, dt = _inproj(x, sh1, sc1, nw1, w_in_p, conv_w[0], conv_b_row, dtb_row,
                                  emit_fz=True, tm=512)
    y_fwd = _ssd(xs, bc, dt, xs_c, bc_c, dt_c, a_row, False, None, rows_per_step=1024)
    dsk_row = jnp.repeat(d_skip[0], SSM_HEAD_DIM)[None]
    y_ssm = _ssd(xs, bc, dt, xs_c, bc_c, dt_c, a_row, True,
                 (y_fwd, z, dsk_row, ssm_norm_w[0][None]), rows_per_step=1024)
    four = _fourier_positions(p, q)

    w_r = jnp.pad(w_router[0], ((0, 0), (0, LANE - N_EXPERTS)))
    x1, h2, probs = _outproj(four, y_ssm, x, g1, sh2, sc2, norm2_w[0][None],
                             w_out[0].astype(BF16), w_r, tm=512)

    cap = CAPACITY_FACTOR * L // N_EXPERTS
    vals, idx = lax.top_k(jnp.swapaxes(probs[..., :N_EXPERTS], 1, 2), cap)
    bidx = jnp.arange(bsz)[:, None, None]
    xg = h2[bidx, idx]
    y = _expert_ffn(xg, vals[..., None], w_gate[0], w_up[0], w_down[0])
    return _combine(idx[:, :, None, :], y, x1, g2, final_norm_w[None], tb=1024, e_blk=8)
```

```python
import functools

import jax
import jax.numpy as jnp
from jax import lax
from jax.experimental import pallas as pl
from jax.experimental.pallas import tpu as pltpu

D_MODEL = 1024
GRID_W = 64
F_GROUP_DIM = 64
F_DIM = D_MODEL // 2
F_GROUPS = F_DIM // F_GROUP_DIM
SSM_HEAD_DIM = 64
D_SSM = 3 * D_MODEL // 2
SSM_HEADS = D_SSM // SSM_HEAD_DIM
SSM_GROUPS = 4
D_STATE = 128
CONV_K = 5
CHUNK = 128
D_MIX = F_DIM + D_SSM
CONV_DIM = D_SSM + 2 * SSM_GROUPS * D_STATE
D_IN_PROJ = F_DIM + D_SSM + CONV_DIM + 2 * SSM_HEADS
N_EXPERTS = 16
CAPACITY_FACTOR = 2
D_EXPERT = 2048
N_MOD = 6
EPS = 1e-6

LANE = 128
DT_PAD = LANE
VMEM_LIMIT = 56 * 1024 * 1024

F32 = jnp.float32
BF16 = jnp.bfloat16


def _norm_mod(x, nw, shift, scale):
    ms = jnp.mean(x * x, axis=-1, keepdims=True)
    y = x * lax.rsqrt(ms + EPS) * nw
    return y * (1.0 + scale) + shift


HALO = 8
Z0 = 2 * F_DIM
XBC0 = Z0 + D_SSM
DT0 = XBC0 + CONV_DIM
CONV_COLS = 512


def _softplus(v):
    return jnp.maximum(v, 0.0) + jnp.log1p(jnp.exp(-jnp.abs(v)))


def _inproj_kernel(x_ref, xp_ref, xn_ref, sh_ref, sc_ref, nw_ref, w_ref, cw_ref, cb_ref, dtb_ref,
                   *refs, emit_fz):
    if emit_fz:
        p_ref, q_ref, z_ref, xs_ref, bc_ref, dt_ref, s_ref = refs
    else:
        xs_ref, bc_ref, dt_ref, s_ref = refs
    i = pl.program_id(1)
    tm = x_ref.shape[1]
    nw, sh, sc = nw_ref[...], sh_ref[0], sc_ref[0]
    h = _norm_mod(x_ref[0], nw, sh, sc).astype(BF16)
    halo = jnp.concatenate([xp_ref[0], xn_ref[0]], axis=0)
    hh = _norm_mod(halo, nw, sh, sc).astype(BF16)
    dt_raw = jnp.dot(h, w_ref[:, DT0:DT0 + DT_PAD], preferred_element_type=F32)
    dt_ref[0] = _softplus(dt_raw + dtb_ref[...])

    def preconv(ci):
        wc = w_ref[:, XBC0 + ci * CONV_COLS:XBC0 + (ci + 1) * CONV_COLS]
        s_ref[ci, HALO:HALO + tm, :] = jnp.dot(h, wc, preferred_element_type=F32)
        hal = jnp.dot(hh, wc, preferred_element_type=F32)
        s_ref[ci, 0:HALO, :] = jnp.where(i > 0, hal[:HALO], 0.0)
        s_ref[ci, HALO + tm:, :] = jnp.where(i < pl.num_programs(1) - 1, hal[HALO:], 0.0)

    plain = []
    if emit_fz:
        plain = [(p_ref, 0, 0), (q_ref, 0, F_DIM)] + [
            (z_ref, c, Z0 + c) for c in range(0, D_SSM, CONV_COLS)]
    n_conv = CONV_DIM // CONV_COLS
    preconv(0)
    for ci, c0 in enumerate(range(0, CONV_DIM, CONV_COLS)):
        if ci + 1 < n_conv:
            preconv(ci + 1)
        for o_ref, oc, wc0 in plain[ci::n_conv]:
            o_ref[0, :, oc:oc + CONV_COLS] = jnp.dot(
                h, w_ref[:, wc0:wc0 + CONV_COLS], preferred_element_type=F32).astype(BF16)
        acc = jnp.broadcast_to(cb_ref[:, c0:c0 + CONV_COLS], (tm, CONV_COLS))
        for k in range(CONV_K):
            acc = acc + cw_ref[k:k + 1, c0:c0 + CONV_COLS] * s_ref[
                ci, pl.ds(HALO - CONV_K // 2 + k, tm), :]
        act = (acc * jax.nn.sigmoid(acc)).astype(BF16)
        if c0 < D_SSM:
            xs_ref[0, :, c0:c0 + CONV_COLS] = act
        else:
            bc_ref[0, :, c0 - D_SSM:c0 - D_SSM + CONV_COLS] = act


def _inproj(x, shift, scale, nw, w, conv_w, conv_b, dt_bias, emit_fz, tm):
    bsz, L, D = x.shape
    n_mod = shift.shape[0]
    mod_map = (lambda b, i: (b, 0, 0)) if n_mod > 1 else (lambda b, i: (0, 0, 0))
    const = lambda b, i: (0, 0)
    row = lambda n: pl.BlockSpec((1, tm, n), lambda b, i: (b, i, 0))
    hb = tm // HALO
    widths = ([F_DIM, F_DIM, D_SSM] if emit_fz else []) + [D_SSM, 2 * SSM_GROUPS * D_STATE, DT_PAD]
    dtypes = ([BF16, BF16, BF16] if emit_fz else []) + [BF16, BF16, F32]
    return pl.pallas_call(
        functools.partial(_inproj_kernel, emit_fz=emit_fz),
        grid=(bsz, L // tm),
        in_specs=[
            row(D),
            pl.BlockSpec((1, HALO, D), lambda b, i: (b, jnp.maximum(i * hb - 1, 0), 0)),
            pl.BlockSpec((1, HALO, D), lambda b, i: (b, jnp.minimum((i + 1) * hb, L // HALO - 1), 0)),
            pl.BlockSpec((1, 1, D), mod_map),
            pl.BlockSpec((1, 1, D), mod_map),
            pl.BlockSpec((1, D), const),
            pl.BlockSpec(w.shape, const),
            pl.BlockSpec(conv_w.shape, const),
            pl.BlockSpec(conv_b.shape, const),
            pl.BlockSpec(dt_bias.shape, const),
        ],
        out_specs=[row(n) for n in widths],
        out_shape=[jax.ShapeDtypeStruct((bsz, L, n), dt) for n, dt in zip(widths, dtypes)],
        scratch_shapes=[pltpu.VMEM((CONV_DIM // CONV_COLS, tm + 2 * HALO, CONV_COLS), F32)],
        compiler_params=pltpu.CompilerParams(
            dimension_semantics=("parallel", "arbitrary"),
            vmem_limit_bytes=VMEM_LIMIT),
        name="inproj",
    )(x, x, x, shift, scale, nw, w, conv_w, conv_b, dt_bias)


HEADS_PER_GROUP = SSM_HEADS // SSM_GROUPS
GROUP_W = HEADS_PER_GROUP * SSM_HEAD_DIM
PAIRS_PER_GROUP = GROUP_W // LANE


def _ssd_chunk(ldx, ldbc, dt, a_row, ex_ref, st_ref, rt_ref, reverse, emit):
    off = SSM_HEADS if reverse else 0
    end = 0 if reverse else CHUNK - 1
    r = lax.broadcasted_iota(jnp.int32, (CHUNK, CHUNK), 0)
    s = lax.broadcasted_iota(jnp.int32, (CHUNK, CHUNK), 1)
    keep = (r <= s) if reverse else (r >= s)
    left = lax.broadcasted_iota(jnp.int32, (CHUNK, LANE), 1) < SSM_HEAD_DIM
    left_bf = left.astype(BF16)
    right_bf = 1 - left_bf

    def expand_row(v, j):
        return jnp.where(left[:1], v[:, j:j + 1], v[:, j + 1:j + 2])

    dA = dt * a_row
    tri = keep.astype(BF16)
    p1 = dA.astype(BF16)
    r1 = dA - p1.astype(F32)
    p2 = r1.astype(BF16)
    p3 = (r1 - p2.astype(F32)).astype(BF16)
    acs = (jnp.dot(tri, p1, preferred_element_type=F32)
           + jnp.dot(tri, p2, preferred_element_type=F32)
           + jnp.dot(tri, p3, preferred_element_type=F32))
    yield
    acs_end = acs[end:end + 1, :]
    w_state = jnp.exp(acs_end - acs) * dt
    dec_row = jnp.exp(acs_end)
    if emit is not None:
        rt_ref[...] = (jnp.log(dt) - acs).T
        e_in = jnp.exp(acs)
    def expand(v, j):
        return jnp.take_along_axis(v, jnp.where(left, j, j + 1), axis=1,
                                   mode="promise_in_bounds")

    w_wide = jnp.dot(w_state.astype(BF16), ex_ref[...], preferred_element_type=F32)
    yield

    for g in range(SSM_GROUPS):
        bg = ldbc(D_STATE * g, D_STATE)
        cg = ldbc(SSM_GROUPS * D_STATE + D_STATE * g, D_STATE)
        hst = st_ref[g]
        if emit is not None:
            cb = lax.dot_general(cg, bg, (((1,), (1,)), ((), ())), preferred_element_type=F32)
            y_off = jnp.dot(cg, hst.astype(BF16), preferred_element_type=F32)
        xw, dec, ys = [], [], []
        for p in range(PAIRS_PER_GROUP):
            j = off + HEADS_PER_GROUP * g + 2 * p
            c0 = GROUP_W * g + LANE * p
            xp = ldx(c0, LANE)
            xw.append((xp.astype(F32) * w_wide[:, c0:c0 + LANE]).astype(BF16))
            dec.append(expand_row(dec_row, j))
            if emit is not None:
                ms = []
                for jj in (j, j + 1):
                    seg = acs[:, jj:jj + 1] + rt_ref[jj:jj + 1, :]
                    ms.append((cb * jnp.exp(jnp.where(keep, seg, -jnp.inf))).astype(BF16))
                lhs = jnp.concatenate(ms, axis=1)
                rhs = jnp.concatenate([xp * left_bf, xp * right_bf], axis=0)
                y_d = jnp.dot(lhs, rhs, preferred_element_type=F32)
                ys.append(y_d + y_off[:, LANE * p:LANE * (p + 1)] * expand(e_in, j))
            yield
        upd = lax.dot_general(bg, jnp.concatenate(xw, axis=1), (((0,), (0,)), ((), ())),
                              preferred_element_type=F32)
        st_ref[g] = hst * jnp.concatenate(dec, axis=1) + upd
        if emit is not None:
            emit(g, jnp.concatenate(ys, axis=1))
        yield


def _interleave(gens):
    gens = list(gens)
    while gens:
        for gen in list(gens):
            try:
                next(gen)
            except StopIteration:
                gens.remove(gen)


def _ssd_kernel(*refs, reverse, final):
    if final:
        (xs_ref, bc_ref, dt_ref, xsc_ref, bcc_ref, dtc_ref, ar_ref, ex_ref,
         yf_ref, z_ref, dsk_ref, gw_ref, o_ref, st_ref, rt_ref) = refs
    else:
        (xs_ref, bc_ref, dt_ref, xsc_ref, bcc_ref, dtc_ref, ar_ref, ex_ref,
         o_ref, st_ref, rt_ref) = refs
    n_seq = xs_ref.shape[0]
    n_chunks = xs_ref.shape[1] // CHUNK
    n_ctx = xsc_ref.shape[1] // CHUNK
    a_row = ar_ref[...]

    @pl.when(pl.program_id(1) == 0)
    def _():
        st_ref[...] = jnp.zeros(st_ref.shape, F32)
        for c in (range(n_ctx - 1, -1, -1) if reverse else range(n_ctx)):
            rows = pl.ds(c * CHUNK, CHUNK)
            _interleave(
                _ssd_chunk(lambda c0, w, n=n: xsc_ref[n, rows, c0:c0 + w],
                           lambda c0, w, n=n: bcc_ref[n, rows, c0:c0 + w],
                           dtc_ref[n, rows, :], a_row, ex_ref, st_ref.at[n], rt_ref.at[n],
                           reverse, None)
                for n in range(n_seq))

    def body(k, carry):
        c = (n_chunks - 1 - k) if reverse else k
        rows = pl.ds(pl.multiple_of(c * CHUNK, CHUNK), CHUNK)

        def emit(n, g, y):
            cols = slice(GROUP_W * g, GROUP_W * (g + 1))
            if final:
                y = y + yf_ref[n, rows, cols].astype(F32)
                y = y + dsk_ref[:, cols] * xs_ref[n, rows, cols].astype(F32)
                z = z_ref[n, rows, cols].astype(F32)
                y = y * (z * jax.nn.sigmoid(z))
                y = y * lax.rsqrt(jnp.mean(y * y, axis=-1, keepdims=True) + EPS) * gw_ref[:, cols]
            o_ref[n, rows, cols] = y.astype(BF16)

        _interleave(
            _ssd_chunk(lambda c0, w, n=n: xs_ref[n, rows, c0:c0 + w],
                       lambda c0, w, n=n: bc_ref[n, rows, c0:c0 + w],
                       dt_ref[n, rows, :], a_row, ex_ref, st_ref.at[n], rt_ref.at[n],
                       reverse, functools.partial(emit, n))
            for n in range(n_seq))
        return carry

    lax.fori_loop(0, n_chunks, body, 0)


def _ssd(xs, bc, dt, xs_c, bc_c, dt_c, a_row, reverse, extra, rows_per_step, n_seq):
    bsz, L, _ = xs.shape
    n_blk = L // rows_per_step
    blk = (lambda b, i: (b, n_blk - 1 - i, 0)) if reverse else (lambda b, i: (b, i, 0))
    row = lambda n: pl.BlockSpec((n_seq, rows_per_step, n), blk)
    ctx = lambda a: pl.BlockSpec((n_seq,) + a.shape[1:], lambda b, i: (b, 0, 0))
    const = lambda a: pl.BlockSpec(a.shape, lambda b, i: (0, 0))
    head_of_lane = (SSM_HEADS if reverse else 0) + jnp.arange(D_SSM) // SSM_HEAD_DIM
    ex = (jnp.arange(DT_PAD)[:, None] == head_of_lane[None, :]).astype(BF16)
    in_specs = [row(D_SSM), row(bc.shape[2]), row(DT_PAD), ctx(xs_c), ctx(bc_c), ctx(dt_c),
                const(a_row), const(ex)]
    args = [xs, bc, dt, xs_c, bc_c, dt_c, a_row, ex]
    if extra is not None:
        y_fwd, z, dsk, gw = extra
        in_specs += [row(D_SSM), row(D_SSM), const(dsk), const(gw)]
        args += [y_fwd, z, dsk, gw]
    return pl.pallas_call(
        functools.partial(_ssd_kernel, reverse=reverse, final=extra is not None),
        grid=(bsz // n_seq, n_blk),
        in_specs=in_specs,
        out_specs=row(D_SSM),
        out_shape=jax.ShapeDtypeStruct((bsz, L, D_SSM), BF16),
        scratch_shapes=[pltpu.VMEM((n_seq, SSM_GROUPS, D_STATE, GROUP_W), F32),
                        pltpu.VMEM((n_seq, DT_PAD, CHUNK), F32)],
        compiler_params=pltpu.CompilerParams(
            dimension_semantics=("parallel", "arbitrary"),
            vmem_limit_bytes=VMEM_LIMIT),
        name="ssd_bwd" if reverse else "ssd_fwd",
    )(*args)


def _outproj_kernel(f_ref, y_ref, x_ref, g1_ref, sh_ref, sc_ref, nw_ref, wo_ref, wr_ref,
                    x1_ref, h2_ref, p_ref):
    m = jnp.dot(f_ref[0], wo_ref[:F_DIM, :], preferred_element_type=F32)
    m = m + jnp.dot(y_ref[0], wo_ref[F_DIM:, :], preferred_element_type=F32)
    x1 = x_ref[0] + g1_ref[0] * m
    x1_ref[0] = x1
    h2 = _norm_mod(x1, nw_ref[...], sh_ref[0], sc_ref[0])
    h2_ref[0] = h2.astype(BF16)
    logits = jnp.dot(h2, wr_ref[...], preferred_element_type=F32)
    lane = lax.broadcasted_iota(jnp.int32, logits.shape, 1)
    logits = jnp.where(lane < N_EXPERTS, logits, -jnp.inf)
    e = jnp.exp(logits - jnp.max(logits, axis=-1, keepdims=True))
    p_ref[0] = e / jnp.sum(e, axis=-1, keepdims=True)


def _outproj(four, yssm, x, g1, sh2, sc2, nw, wo, wr, tm):
    bsz, L, D = x.shape
    row = lambda n: pl.BlockSpec((1, tm, n), lambda b, i: (b, i, 0))
    mod = pl.BlockSpec((1, 1, D), lambda b, i: (b, 0, 0))
    return pl.pallas_call(
        _outproj_kernel,
        grid=(bsz, L // tm),
        in_specs=[row(F_DIM), row(D_SSM), row(D), mod, mod, mod,
                  pl.BlockSpec((1, D), lambda b, i: (0, 0)),
                  pl.BlockSpec(wo.shape, lambda b, i: (0, 0)),
                  pl.BlockSpec(wr.shape, lambda b, i: (0, 0))],
        out_specs=[row(D), row(D), row(LANE)],
        out_shape=[jax.ShapeDtypeStruct((bsz, L, D), F32),
                   jax.ShapeDtypeStruct((bsz, L, D), BF16),
                   jax.ShapeDtypeStruct((bsz, L, LANE), F32)],
        compiler_params=pltpu.CompilerParams(
            dimension_semantics=("parallel", "parallel"),
            vmem_limit_bytes=VMEM_LIMIT),
        name="outproj",
    )(four, yssm, x, g1, sh2, sc2, nw, wo, wr)


FFN_SLABS = 2
FFN_SLAB = D_EXPERT // FFN_SLABS
FFN_COLS = 512


def _ffn_kernel(x_ref, v_ref, wg_ref, wu_ref, wd_ref, o_ref, wgb_ref, wub_ref, wdb_ref, acc_ref):
    b, f = pl.program_id(1), pl.program_id(2)

    @pl.when(b == 0)
    def _():
        wgb_ref[f] = wg_ref[0].astype(BF16)
        wub_ref[f] = wu_ref[0].astype(BF16)
        wdb_ref[f] = wd_ref[0].astype(BF16)

    x = x_ref[0, 0]
    part = jnp.zeros(acc_ref.shape, F32)
    for c in range(0, FFN_SLAB, FFN_COLS):
        g = jnp.dot(x, wgb_ref[f, :, c:c + FFN_COLS], preferred_element_type=F32)
        u = jnp.dot(x, wub_ref[f, :, c:c + FFN_COLS], preferred_element_type=F32)
        a = (g * jax.nn.sigmoid(g) * u).astype(BF16)
        part = part + jnp.dot(a, wdb_ref[f, c:c + FFN_COLS, :], preferred_element_type=F32)

    @pl.when(f == 0)
    def _():
        acc_ref[...] = part

    @pl.when(f > 0)
    def _():
        acc_ref[...] += part

    @pl.when(f == FFN_SLABS - 1)
    def _():
        o_ref[0, 0] = (acc_ref[...] * v_ref[0, 0]).astype(o_ref.dtype)


def _expert_ffn(xg, vals, wg, wu, wd):
    bsz, n_e, cap, D = xg.shape
    slab = lambda b, f: jnp.where(b == 0, f, FFN_SLABS - 1)
    return pl.pallas_call(
        _ffn_kernel,
        grid=(n_e, bsz, FFN_SLABS),
        in_specs=[
            pl.BlockSpec((1, 1, cap, D), lambda e, b, f: (b, e, 0, 0)),
            pl.BlockSpec((1, 1, cap, 1), lambda e, b, f: (b, e, 0, 0)),
            pl.BlockSpec((1, D, FFN_SLAB), lambda e, b, f: (e, 0, slab(b, f))),
            pl.BlockSpec((1, D, FFN_SLAB), lambda e, b, f: (e, 0, slab(b, f))),
            pl.BlockSpec((1, FFN_SLAB, D), lambda e, b, f: (e, slab(b, f), 0)),
        ],
        out_specs=pl.BlockSpec((1, 1, cap, D), lambda e, b, f: (b, e, 0, 0)),
        out_shape=jax.ShapeDtypeStruct((bsz, n_e, cap, D), BF16),
        scratch_shapes=[pltpu.VMEM((FFN_SLABS, D, FFN_SLAB), BF16),
                        pltpu.VMEM((FFN_SLABS, D, FFN_SLAB), BF16),
                        pltpu.VMEM((FFN_SLABS, FFN_SLAB, D), BF16),
                        pltpu.VMEM((cap, D), F32)],
        compiler_params=pltpu.CompilerParams(
            dimension_semantics=("arbitrary", "arbitrary", "arbitrary"),
            vmem_limit_bytes=VMEM_LIMIT),
        name="expert_ffn",
    )(xg, vals, wg, wu, wd)


def _combine_kernel(idx_ref, y_ref, x1_ref, g_ref, nw_ref, o_ref, acc_ref):
    e = pl.program_id(2)
    tb = x1_ref.shape[1]
    n_e, cap, d = y_ref.shape[1:]

    @pl.when(e == 0)
    def _():
        acc_ref[...] = jnp.zeros(acc_ref.shape, F32)

    tok = pl.program_id(1) * tb + lax.broadcasted_iota(jnp.int32, (tb, cap), 0)
    onehot = jnp.concatenate([(tok == idx_ref[0, k]).astype(BF16) for k in range(n_e)], axis=1)
    acc_ref[...] += jnp.dot(onehot, y_ref[0].reshape(n_e * cap, d), preferred_element_type=F32)

    @pl.when(e == pl.num_programs(2) - 1)
    def _():
        x = x1_ref[0] + g_ref[0] * acc_ref[...]
        ms = jnp.mean(x * x, axis=-1, keepdims=True)
        o_ref[0] = x * lax.rsqrt(ms + EPS) * nw_ref[...]


def _combine(idx, y, x1, g2, nw, tb, e_blk):
    bsz, L, D = x1.shape
    n_e, cap = y.shape[1], y.shape[2]
    row = pl.BlockSpec((1, tb, D), lambda b, t, e: (b, t, 0))
    return pl.pallas_call(
        _combine_kernel,
        grid=(bsz, L // tb, n_e // e_blk),
        in_specs=[pl.BlockSpec((1, e_blk, 1, cap), lambda b, t, e: (b, e, 0, 0)),
                  pl.BlockSpec((1, e_blk, cap, D), lambda b, t, e: (b, e, 0, 0)),
                  row,
                  pl.BlockSpec((1, 1, D), lambda b, t, e: (b, 0, 0)),
                  pl.BlockSpec((1, D), lambda b, t, e: (0, 0))],
        out_specs=row,
        out_shape=jax.ShapeDtypeStruct((bsz, L, D), F32),
        scratch_shapes=[pltpu.VMEM((tb, D), F32)],
        compiler_params=pltpu.CompilerParams(
            dimension_semantics=("parallel", "parallel", "arbitrary"),
            vmem_limit_bytes=VMEM_LIMIT),
        name="combine",
    )(idx, y, x1, g2, nw)


DFT_LANES = 8 * F_DIM
DFT_ROWS = 8


def _dft1_kernel(p_ref, q_ref, mh_ref, ml_ref, tc_ref, ts_ref, o_ref):
    w = GRID_W
    v = jnp.concatenate([p_ref[0], q_ref[0]], axis=0)
    y = (jnp.dot(mh_ref[...], v, preferred_element_type=F32)
         + jnp.dot(ml_ref[...], v, preferred_element_type=F32))
    yr, yi = y[:w], y[w:]
    tc, ts = tc_ref[...], ts_ref[...]
    o_ref[0, :w] = (yr * tc + yi * ts).astype(BF16)
    o_ref[0, w:] = (yi * tc - yr * ts).astype(BF16)


def _dft2_kernel(yr_ref, yi_ref, mh_ref, ml_ref, o_ref):
    for k in range(DFT_ROWS):
        v = jnp.concatenate([yr_ref[0, k], yi_ref[0, k]], axis=0)
        o = (jnp.dot(mh_ref[...], v, preferred_element_type=F32)
             + jnp.dot(ml_ref[...], v, preferred_element_type=F32))
        o_ref[0, :, k * F_DIM:(k + 1) * F_DIM] = o.astype(BF16)


def _split_bf16(m):
    hi = m.astype(BF16)
    return hi, (m - hi.astype(F32)).astype(BF16)


def _fourier_positions(p, q):
    bsz, L, _ = p.shape
    w = GRID_W
    wide = w * F_DIM
    k = jnp.arange(w, dtype=F32)
    ang = (2.0 * jnp.pi / w) * jnp.outer(k, k)
    c, s = jnp.cos(ang), jnp.sin(ang)
    m1h, m1l = _split_bf16(jnp.block([[c, -s], [-s, -c]]))
    m2h, m2l = _split_bf16(jnp.concatenate([c, s], axis=1))
    tw = (2.0 * jnp.pi / L) * jnp.outer(k, k)
    tc = jnp.repeat(jnp.cos(tw), F_DIM, axis=1)
    ts = jnp.repeat(jnp.sin(tw), F_DIM, axis=1)
    const = lambda a: pl.BlockSpec(a.shape, lambda b, i: (0, 0))
    y1 = pl.pallas_call(
        _dft1_kernel,
        grid=(bsz, wide // DFT_LANES),
        in_specs=[pl.BlockSpec((1, w, DFT_LANES), lambda b, i: (b, 0, i)),
                  pl.BlockSpec((1, w, DFT_LANES), lambda b, i: (b, 0, i)),
                  const(m1h), const(m1l),
                  pl.BlockSpec((w, DFT_LANES), lambda b, i: (0, i)),
                  pl.BlockSpec((w, DFT_LANES), lambda b, i: (0, i))],
        out_specs=pl.BlockSpec((1, 2 * w, DFT_LANES), lambda b, i: (b, 0, i)),
        out_shape=jax.ShapeDtypeStruct((bsz, 2 * w, wide), BF16),
        compiler_params=pltpu.CompilerParams(
            dimension_semantics=("parallel", "parallel"), vmem_limit_bytes=VMEM_LIMIT),
        name="dft_stage1",
    )(p.reshape(bsz, w, wide), q.reshape(bsz, w, wide), m1h, m1l, tc, ts)
    y1 = y1.reshape(bsz, 2 * w, w, F_DIM)
    n_blk = w // DFT_ROWS
    out = pl.pallas_call(
        _dft2_kernel,
        grid=(bsz, n_blk),
        in_specs=[pl.BlockSpec((1, DFT_ROWS, w, F_DIM), lambda b, i: (b, i, 0, 0)),
                  pl.BlockSpec((1, DFT_ROWS, w, F_DIM), lambda b, i: (b, n_blk + i, 0, 0)),
                  const(m2h), const(m2l)],
        out_specs=pl.BlockSpec((1, w, DFT_ROWS * F_DIM), lambda b, i: (b, 0, i)),
        out_shape=jax.ShapeDtypeStruct((bsz, w, wide), BF16),
        compiler_params=pltpu.CompilerParams(
            dimension_semantics=("parallel", "parallel"), vmem_limit_bytes=VMEM_LIMIT),
        name="dft_stage2",
    )(y1, y1, m2h, m2l)
    return out.reshape(bsz, L, F_DIM)


def _fourier_fold(w_uf, w_fourier):
    n = F_GROUP_DIM
    k = jnp.arange(n, dtype=F32)
    ang = (2.0 * jnp.pi / n) * jnp.outer(k, k)
    hi = lax.Precision.HIGHEST
    scale = 1.0 / jnp.sqrt(jnp.float32(GRID_W * GRID_W * n))
    a = jnp.einsum('cm,gmj->gcj', jnp.cos(ang), w_fourier, precision=hi) * scale
    b = jnp.einsum('cm,gmj->gcj', jnp.sin(ang), w_fourier, precision=hi) * scale
    wg = w_uf.reshape(-1, F_GROUPS, n)
    wp = jnp.einsum('dgc,gcj->dgj', wg, a, precision=hi).reshape(-1, F_DIM)
    wq = jnp.einsum('dgc,gcj->dgj', wg, b, precision=hi).reshape(-1, F_DIM)
    return wp, wq


def kernel(x, c, ctx, c_ctx, w_ada, b_ada, norm1_w, w_in, conv_w, conv_b, dt_bias, a_log,
           d_skip, ssm_norm_w, w_fourier, w_out, norm2_w, w_router, w_gate, w_up, w_down,
           final_norm_w):
    bsz, L, D = x.shape
    hi = lax.Precision.HIGHEST
    mod_l = (jnp.dot(jax.nn.silu(c), w_ada[0], precision=hi) + b_ada[0]).reshape(bsz, N_MOD, 1, D)
    mod_c = (jnp.dot(jax.nn.silu(c_ctx), w_ada[0], precision=hi) + b_ada[0]).reshape(N_MOD, 1, 1, D)
    sh1, sc1, g1, sh2, sc2, g2 = [mod_l[:, i] for i in range(N_MOD)]
    csh1, csc1 = mod_c[0], mod_c[1]

    dt_pad = DT_PAD - 2 * SSM_HEADS
    assert L == GRID_W * GRID_W
    w_p, w_q = _fourier_fold(w_in[0][:, :F_DIM], w_fourier[0])
    w_in_p = jnp.concatenate(
        [w_p, w_q, w_in[0][:, F_DIM:], jnp.zeros((D, dt_pad), F32)], axis=1).astype(BF16)
    dtb_row = jnp.pad(dt_bias[0].reshape(1, 2 * SSM_HEADS), ((0, 0), (0, dt_pad)))
    a_row = jnp.pad(-jnp.exp(a_log[0]).reshape(1, 2 * SSM_HEADS), ((0, 0), (0, dt_pad)))
    nw1 = norm1_w[0][None]
    conv_b_row = conv_b[0][None]

    xs_c, bc_c, dt_c = _inproj(ctx, csh1, csc1, nw1, w_in_p, conv_w[0], conv_b_row, dtb_row,
                               emit_fz=False, tm=ctx.shape[1])
    p, q, z, xs, bc, dt = _inproj(x, sh1, sc1, nw1, w_in_p, conv_w[0], conv_b_row, dtb_row,
                                  emit_fz=True, tm=512)
    y_fwd = _ssd(xs, bc, dt, xs_c, bc_c, dt_c, a_row, False, None, rows_per_step=512, n_seq=2)
    dsk_row = jnp.repeat(d_skip[0], SSM_HEAD_DIM)[None]
    y_ssm = _ssd(xs, bc, dt, xs_c, bc_c, dt_c, a_row, True,
                 (y_fwd, z, dsk_row, ssm_norm_w[0][None]), rows_per_step=512, n_seq=2)
    four = _fourier_positions(p, q)

    w_r = jnp.pad(w_router[0], ((0, 0), (0, LANE - N_EXPERTS)))
    x1, h2, probs = _outproj(four, y_ssm, x, g1, sh2, sc2, norm2_w[0][None],
                             w_out[0].astype(BF16), w_r, tm=512)

    cap = CAPACITY_FACTOR * L // N_EXPERTS
    vals, idx = lax.top_k(jnp.swapaxes(probs[..., :N_EXPERTS], 1, 2), cap)
    bidx = jnp.arange(bsz)[:, None, None]
    xg = h2[bidx, idx]
    y = _expert_ffn(xg, vals[..., None], w_gate[0], w_up[0], w_down[0])
    return _combine(idx[:, :, None, :], y, x1, g2, final_norm_w[None], tb=1024, e_blk=8)
```

```python
import functools

import jax
import jax.numpy as jnp
from jax import lax
from jax.experimental import pallas as pl
from jax.experimental.pallas import tpu as pltpu

D_MODEL = 1024
GRID_W = 64
F_GROUP_DIM = 64
F_DIM = D_MODEL // 2
F_GROUPS = F_DIM // F_GROUP_DIM
SSM_HEAD_DIM = 64
D_SSM = 3 * D_MODEL // 2
SSM_HEADS = D_SSM // SSM_HEAD_DIM
SSM_GROUPS = 4
D_STATE = 128
CONV_K = 5
CHUNK = 128
D_MIX = F_DIM + D_SSM
CONV_DIM = D_SSM + 2 * SSM_GROUPS * D_STATE
D_IN_PROJ = F_DIM + D_SSM + CONV_DIM + 2 * SSM_HEADS
N_EXPERTS = 16
CAPACITY_FACTOR = 2
D_EXPERT = 2048
N_MOD = 6
EPS = 1e-6

LANE = 128
DT_PAD = LANE
VMEM_LIMIT = 56 * 1024 * 1024

F32 = jnp.float32
BF16 = jnp.bfloat16


def _norm_mod(x, nw, shift, scale):
    ms = jnp.mean(x * x, axis=-1, keepdims=True)
    y = x * lax.rsqrt(ms + EPS) * nw
    return y * (1.0 + scale) + shift


HALO = 8
Z0 = 2 * F_DIM
XBC0 = Z0 + D_SSM
DT0 = XBC0 + CONV_DIM
CONV_COLS = 512
CONV_ROWS = 128
MM_COLS = 256


def _softplus(v):
    return jnp.maximum(v, 0.0) + jnp.log1p(jnp.exp(-jnp.abs(v)))


def _inproj_kernel(x_ref, xp_ref, xn_ref, sh_ref, sc_ref, nw_ref, w_ref, cw_ref, cb_ref, dtb_ref,
                   *refs, emit_fz):
    n_conv = CONV_DIM // CONV_COLS
    s_refs = refs[-n_conv:]
    if emit_fz:
        p_ref, q_ref, z_ref, xs_ref, bc_ref, dt_ref = refs[:-n_conv]
    else:
        xs_ref, bc_ref, dt_ref = refs[:-n_conv]
    i = pl.program_id(1)
    tm = x_ref.shape[1]
    nw, sh, sc = nw_ref[...], sh_ref[0], sc_ref[0]
    h = _norm_mod(x_ref[0], nw, sh, sc).astype(BF16)
    halo = jnp.concatenate([xp_ref[0], xn_ref[0]], axis=0)
    hh = _norm_mod(halo, nw, sh, sc).astype(BF16)
    dt_raw = jnp.dot(h, w_ref[:, DT0:DT0 + DT_PAD], preferred_element_type=F32)
    dt_ref[0] = _softplus(dt_raw + dtb_ref[...])

    def preconv(ci, half):
        c = half * MM_COLS
        wc = w_ref[:, XBC0 + ci * CONV_COLS + c:XBC0 + ci * CONV_COLS + c + MM_COLS]
        s_ref = s_refs[ci]
        s_ref[HALO:HALO + tm, c:c + MM_COLS] = jnp.dot(h, wc, preferred_element_type=F32)
        hal = jnp.dot(hh, wc, preferred_element_type=F32)
        s_ref[0:HALO, c:c + MM_COLS] = jnp.where(i > 0, hal[:HALO], 0.0)
        s_ref[HALO + tm:, c:c + MM_COLS] = jnp.where(
            i < pl.num_programs(1) - 1, hal[HALO:], 0.0)

    def plain(o_ref, oc, wc0):
        o_ref[0, :, oc:oc + MM_COLS] = jnp.dot(
            h, w_ref[:, wc0:wc0 + MM_COLS], preferred_element_type=F32).astype(BF16)

    def conv(ci, r0):
        c0 = ci * CONV_COLS
        acc = jnp.broadcast_to(cb_ref[:, c0:c0 + CONV_COLS], (CONV_ROWS, CONV_COLS))
        for k in range(CONV_K):
            acc = acc + cw_ref[k:k + 1, c0:c0 + CONV_COLS] * s_refs[ci][
                pl.ds(r0 + HALO - CONV_K // 2 + k, CONV_ROWS), :]
        act = (acc * jax.nn.sigmoid(acc)).astype(BF16)
        if c0 < D_SSM:
            xs_ref[0, r0:r0 + CONV_ROWS, c0:c0 + CONV_COLS] = act
        else:
            bc_ref[0, r0:r0 + CONV_ROWS, c0 - D_SSM:c0 - D_SSM + CONV_COLS] = act

    n_conv = CONV_DIM // CONV_COLS
    halves = CONV_COLS // MM_COLS
    mm = []
    for ci in range(n_conv):
        if ci + 1 < n_conv:
            mm += [functools.partial(preconv, ci + 1, hf) for hf in range(halves)]
    if emit_fz:
        extra = ([(p_ref, c, c) for c in range(0, F_DIM, MM_COLS)]
                 + [(q_ref, c, F_DIM + c) for c in range(0, F_DIM, MM_COLS)]
                 + [(z_ref, c, Z0 + c) for c in range(0, D_SSM, MM_COLS)])
        per = -(-len(extra) // n_conv)
        merged = []
        for ci in range(n_conv):
            merged += mm[ci * halves:(ci + 1) * halves]
            merged += [functools.partial(plain, *e) for e in extra[ci * per:(ci + 1) * per]]
        mm = merged
    for hf in range(halves):
        preconv(0, hf)
    blocks = [(ci, r0) for ci in range(n_conv) for r0 in range(0, tm, CONV_ROWS)]
    per_block = -(-len(mm) // len(blocks))
    for bi, (ci, r0) in enumerate(blocks):
        for piece in mm[bi * per_block:(bi + 1) * per_block]:
            piece()
        conv(ci, r0)


def _inproj(x, shift, scale, nw, w, conv_w, conv_b, dt_bias, emit_fz, tm):
    bsz, L, D = x.shape
    n_mod = shift.shape[0]
    mod_map = (lambda b, i: (b, 0, 0)) if n_mod > 1 else (lambda b, i: (0, 0, 0))
    const = lambda b, i: (0, 0)
    row = lambda n: pl.BlockSpec((1, tm, n), lambda b, i: (b, i, 0))
    hb = tm // HALO
    widths = ([F_DIM, F_DIM, D_SSM] if emit_fz else []) + [D_SSM, 2 * SSM_GROUPS * D_STATE, DT_PAD]
    dtypes = ([BF16, BF16, BF16] if emit_fz else []) + [BF16, BF16, F32]
    return pl.pallas_call(
        functools.partial(_inproj_kernel, emit_fz=emit_fz),
        grid=(bsz, L // tm),
        in_specs=[
            row(D),
            pl.BlockSpec((1, HALO, D), lambda b, i: (b, jnp.maximum(i * hb - 1, 0), 0)),
            pl.BlockSpec((1, HALO, D), lambda b, i: (b, jnp.minimum((i + 1) * hb, L // HALO - 1), 0)),
            pl.BlockSpec((1, 1, D), mod_map),
            pl.BlockSpec((1, 1, D), mod_map),
            pl.BlockSpec((1, D), const),
            pl.BlockSpec(w.shape, const),
            pl.BlockSpec(conv_w.shape, const),
            pl.BlockSpec(conv_b.shape, const),
            pl.BlockSpec(dt_bias.shape, const),
        ],
        out_specs=[row(n) for n in widths],
        out_shape=[jax.ShapeDtypeStruct((bsz, L, n), dt) for n, dt in zip(widths, dtypes)],
        scratch_shapes=[pltpu.VMEM((tm + 2 * HALO, CONV_COLS), F32)] * (CONV_DIM // CONV_COLS),
        compiler_params=pltpu.CompilerParams(
            dimension_semantics=("parallel", "arbitrary"),
            vmem_limit_bytes=VMEM_LIMIT),
        name="inproj",
    )(x, x, x, shift, scale, nw, w, conv_w, conv_b, dt_bias)


HEADS_PER_GROUP = SSM_HEADS // SSM_GROUPS
GROUP_W = HEADS_PER_GROUP * SSM_HEAD_DIM
PAIRS_PER_GROUP = GROUP_W // LANE


def _ssd_chunk(ldx, ldbc, dt, a_row, ex_ref, st_ref, rt_ref, reverse, emit):
    off = SSM_HEADS if reverse else 0
    end = 0 if reverse else CHUNK - 1
    r = lax.broadcasted_iota(jnp.int32, (CHUNK, CHUNK), 0)
    s = lax.broadcasted_iota(jnp.int32, (CHUNK, CHUNK), 1)
    keep = (r <= s) if reverse else (r >= s)
    left = lax.broadcasted_iota(jnp.int32, (CHUNK, LANE), 1) < SSM_HEAD_DIM
    left_bf = left.astype(BF16)
    right_bf = 1 - left_bf

    def expand_row(v, j):
        return jnp.where(left[:1], v[:, j:j + 1], v[:, j + 1:j + 2])

    dA = dt * a_row
    tri = keep.astype(BF16)
    p1 = dA.astype(BF16)
    r1 = dA - p1.astype(F32)
    p2 = r1.astype(BF16)
    p3 = (r1 - p2.astype(F32)).astype(BF16)
    acs = (jnp.dot(tri, p1, preferred_element_type=F32)
           + jnp.dot(tri, p2, preferred_element_type=F32)
           + jnp.dot(tri, p3, preferred_element_type=F32))
    yield
    acs_end = acs[end:end + 1, :]
    w_state = jnp.exp(acs_end - acs) * dt
    dec_row = jnp.exp(acs_end)
    if emit is not None:
        rt_ref[...] = (jnp.log(dt) - acs).T
        e_in = jnp.exp(acs)
    def expand(v, j):
        return jnp.take_along_axis(v, jnp.where(left, j, j + 1), axis=1,
                                   mode="promise_in_bounds")

    w_wide = jnp.dot(w_state.astype(BF16), ex_ref[...], preferred_element_type=F32)
    yield

    for g in range(SSM_GROUPS):
        bg = ldbc(D_STATE * g, D_STATE)
        cg = ldbc(SSM_GROUPS * D_STATE + D_STATE * g, D_STATE)
        hst = st_ref[g]
        if emit is not None:
            cb = lax.dot_general(cg, bg, (((1,), (1,)), ((), ())), preferred_element_type=F32)
            y_off = jnp.dot(cg, hst.astype(BF16), preferred_element_type=F32)
        xw, dec, ys = [], [], []
        for p in range(PAIRS_PER_GROUP):
            j = off + HEADS_PER_GROUP * g + 2 * p
            c0 = GROUP_W * g + LANE * p
            xp = ldx(c0, LANE)
            xw.append((xp.astype(F32) * w_wide[:, c0:c0 + LANE]).astype(BF16))
            dec.append(expand_row(dec_row, j))
            if emit is not None:
                ms = []
                for jj in (j, j + 1):
                    seg = acs[:, jj:jj + 1] + rt_ref[jj:jj + 1, :]
                    ms.append((cb * jnp.exp(jnp.where(keep, seg, -jnp.inf))).astype(BF16))
                lhs = jnp.concatenate(ms, axis=1)
                rhs = jnp.concatenate([xp * left_bf, xp * right_bf], axis=0)
                y_d = jnp.dot(lhs, rhs, preferred_element_type=F32)
                ys.append(y_d + y_off[:, LANE * p:LANE * (p + 1)] * expand(e_in, j))
            yield
        upd = lax.dot_general(bg, jnp.concatenate(xw, axis=1), (((0,), (0,)), ((), ())),
                              preferred_element_type=F32)
        st_ref[g] = hst * jnp.concatenate(dec, axis=1) + upd
        if emit is not None:
            emit(g, jnp.concatenate(ys, axis=1))
        yield


def _interleave(gens):
    gens = list(gens)
    while gens:
        for gen in list(gens):
            try:
                next(gen)
            except StopIteration:
                gens.remove(gen)


def _ssd_kernel(*refs, reverse, final):
    if final:
        (xs_ref, bc_ref, dt_ref, xsc_ref, bcc_ref, dtc_ref, ar_ref, ex_ref,
         yf_ref, z_ref, dsk_ref, gw_ref, o_ref, st_ref, rt_ref) = refs
    else:
        (xs_ref, bc_ref, dt_ref, xsc_ref, bcc_ref, dtc_ref, ar_ref, ex_ref,
         o_ref, st_ref, rt_ref) = refs
    n_seq = xs_ref.shape[0]
    n_chunks = xs_ref.shape[1] // CHUNK
    n_ctx = xsc_ref.shape[1] // CHUNK
    a_row = ar_ref[...]

    @pl.when(pl.program_id(1) == 0)
    def _():
        st_ref[...] = jnp.zeros(st_ref.shape, F32)
        for c in (range(n_ctx - 1, -1, -1) if reverse else range(n_ctx)):
            rows = pl.ds(c * CHUNK, CHUNK)
            _interleave(
                _ssd_chunk(lambda c0, w, n=n: xsc_ref[n, rows, c0:c0 + w],
                           lambda c0, w, n=n: bcc_ref[n, rows, c0:c0 + w],
                           dtc_ref[n, rows, :], a_row, ex_ref, st_ref.at[n], rt_ref.at[n],
                           reverse, None)
                for n in range(n_seq))

    def body(k, carry):
        c = (n_chunks - 1 - k) if reverse else k
        rows = pl.ds(pl.multiple_of(c * CHUNK, CHUNK), CHUNK)

        def emit(n, g, y):
            cols = slice(GROUP_W * g, GROUP_W * (g + 1))
            if final:
                y = y + yf_ref[n, rows, cols].astype(F32)
                y = y + dsk_ref[:, cols] * xs_ref[n, rows, cols].astype(F32)
                z = z_ref[n, rows, cols].astype(F32)
                y = y * (z * jax.nn.sigmoid(z))
                y = y * lax.rsqrt(jnp.mean(y * y, axis=-1, keepdims=True) + EPS) * gw_ref[:, cols]
            o_ref[n, rows, cols] = y.astype(BF16)

        _interleave(
            _ssd_chunk(lambda c0, w, n=n: xs_ref[n, rows, c0:c0 + w],
                       lambda c0, w, n=n: bc_ref[n, rows, c0:c0 + w],
                       dt_ref[n, rows, :], a_row, ex_ref, st_ref.at[n], rt_ref.at[n],
                       reverse, functools.partial(emit, n))
            for n in range(n_seq))
        return carry

    lax.fori_loop(0, n_chunks, body, 0)


def _ssd(xs, bc, dt, xs_c, bc_c, dt_c, a_row, reverse, extra, rows_per_step, n_seq):
    bsz, L, _ = xs.shape
    n_blk = L // rows_per_step
    blk = (lambda b, i: (b, n_blk - 1 - i, 0)) if reverse else (lambda b, i: (b, i, 0))
    row = lambda n: pl.BlockSpec((n_seq, rows_per_step, n), blk)
    ctx = lambda a: pl.BlockSpec((n_seq,) + a.shape[1:], lambda b, i: (b, 0, 0))
    const = lambda a: pl.BlockSpec(a.shape, lambda b, i: (0, 0))
    head_of_lane = (SSM_HEADS if reverse else 0) + jnp.arange(D_SSM) // SSM_HEAD_DIM
    ex = (jnp.arange(DT_PAD)[:, None] == head_of_lane[None, :]).astype(BF16)
    in_specs = [row(D_SSM), row(bc.shape[2]), row(DT_PAD), ctx(xs_c), ctx(bc_c), ctx(dt_c),
                const(a_row), const(ex)]
    args = [xs, bc, dt, xs_c, bc_c, dt_c, a_row, ex]
    if extra is not None:
        y_fwd, z, dsk, gw = extra
        in_specs += [row(D_SSM), row(D_SSM), const(dsk), const(gw)]
        args += [y_fwd, z, dsk, gw]
    return pl.pallas_call(
        functools.partial(_ssd_kernel, reverse=reverse, final=extra is not None),
        grid=(bsz // n_seq, n_blk),
        in_specs=in_specs,
        out_specs=row(D_SSM),
        out_shape=jax.ShapeDtypeStruct((bsz, L, D_SSM), BF16),
        scratch_shapes=[pltpu.VMEM((n_seq, SSM_GROUPS, D_STATE, GROUP_W), F32),
                        pltpu.VMEM((n_seq, DT_PAD, CHUNK), F32)],
        compiler_params=pltpu.CompilerParams(
            dimension_semantics=("parallel", "arbitrary"),
            vmem_limit_bytes=VMEM_LIMIT),
        name="ssd_bwd" if reverse else "ssd_fwd",
    )(*args)


def _outproj_kernel(f_ref, y_ref, x_ref, g1_ref, sh_ref, sc_ref, nw_ref, wo_ref, wr_ref,
                    x1_ref, h2_ref, p_ref):
    m = jnp.dot(f_ref[0], wo_ref[:F_DIM, :], preferred_element_type=F32)
    m = m + jnp.dot(y_ref[0], wo_ref[F_DIM:, :], preferred_element_type=F32)
    x1 = x_ref[0] + g1_ref[0] * m
    x1_ref[0] = x1
    h2 = _norm_mod(x1, nw_ref[...], sh_ref[0], sc_ref[0])
    h2_ref[0] = h2.astype(BF16)
    logits = jnp.dot(h2, wr_ref[...], preferred_element_type=F32)
    lane = lax.broadcasted_iota(jnp.int32, logits.shape, 1)
    logits = jnp.where(lane < N_EXPERTS, logits, -jnp.inf)
    e = jnp.exp(logits - jnp.max(logits, axis=-1, keepdims=True))
    p_ref[0] = e / jnp.sum(e, axis=-1, keepdims=True)


def _outproj(four, yssm, x, g1, sh2, sc2, nw, wo, wr, tm):
    bsz, L, D = x.shape
    row = lambda n: pl.BlockSpec((1, tm, n), lambda b, i: (b, i, 0))
    mod = pl.BlockSpec((1, 1, D), lambda b, i: (b, 0, 0))
    return pl.pallas_call(
        _outproj_kernel,
        grid=(bsz, L // tm),
        in_specs=[row(F_DIM), row(D_SSM), row(D), mod, mod, mod,
                  pl.BlockSpec((1, D), lambda b, i: (0, 0)),
                  pl.BlockSpec(wo.shape, lambda b, i: (0, 0)),
                  pl.BlockSpec(wr.shape, lambda b, i: (0, 0))],
        out_specs=[row(D), row(D), row(LANE)],
        out_shape=[jax.ShapeDtypeStruct((bsz, L, D), F32),
                   jax.ShapeDtypeStruct((bsz, L, D), BF16),
                   jax.ShapeDtypeStruct((bsz, L, LANE), F32)],
        compiler_params=pltpu.CompilerParams(
            dimension_semantics=("parallel", "parallel"),
            vmem_limit_bytes=VMEM_LIMIT),
        name="outproj",
    )(four, yssm, x, g1, sh2, sc2, nw, wo, wr)


FFN_SLABS = 2
FFN_SLAB = D_EXPERT // FFN_SLABS
FFN_COLS = 512


def _ffn_kernel(x_ref, wg_ref, wu_ref, wd_ref, o_ref, wgb_ref, wub_ref, wdb_ref, acc_ref):
    b, f = pl.program_id(1), pl.program_id(2)

    @pl.when(b == 0)
    def _():
        wgb_ref[f] = wg_ref[0].astype(BF16)
        wub_ref[f] = wu_ref[0].astype(BF16)
        wdb_ref[f] = wd_ref[0].astype(BF16)

    x = x_ref[0, 0]
    part = jnp.zeros(acc_ref.shape, F32)
    for c in range(0, FFN_SLAB, FFN_COLS):
        g = jnp.dot(x, wgb_ref[f, :, c:c + FFN_COLS], preferred_element_type=F32)
        u = jnp.dot(x, wub_ref[f, :, c:c + FFN_COLS], preferred_element_type=F32)
        a = (g * jax.nn.sigmoid(g) * u).astype(BF16)
        part = part + jnp.dot(a, wdb_ref[f, c:c + FFN_COLS, :], preferred_element_type=F32)

    @pl.when(f == 0)
    def _():
        acc_ref[...] = part

    @pl.when(f > 0)
    def _():
        acc_ref[...] += part

    @pl.when(f == FFN_SLABS - 1)
    def _():
        o_ref[0, 0] = acc_ref[...].astype(o_ref.dtype)


def _expert_ffn(xg, wg, wu, wd):
    bsz, n_e, cap, D = xg.shape
    slab = lambda b, f: jnp.where(b == 0, f, FFN_SLABS - 1)
    return pl.pallas_call(
        _ffn_kernel,
        grid=(n_e, bsz, FFN_SLABS),
        in_specs=[
            pl.BlockSpec((1, 1, cap, D), lambda e, b, f: (b, e, 0, 0)),
            pl.BlockSpec((1, D, FFN_SLAB), lambda e, b, f: (e, 0, slab(b, f))),
            pl.BlockSpec((1, D, FFN_SLAB), lambda e, b, f: (e, 0, slab(b, f))),
            pl.BlockSpec((1, FFN_SLAB, D), lambda e, b, f: (e, slab(b, f), 0)),
        ],
        out_specs=pl.BlockSpec((1, 1, cap, D), lambda e, b, f: (b, e, 0, 0)),
        out_shape=jax.ShapeDtypeStruct((bsz, n_e, cap, D), BF16),
        scratch_shapes=[pltpu.VMEM((FFN_SLABS, D, FFN_SLAB), BF16),
                        pltpu.VMEM((FFN_SLABS, D, FFN_SLAB), BF16),
                        pltpu.VMEM((FFN_SLABS, FFN_SLAB, D), BF16),
                        pltpu.VMEM((cap, D), F32)],
        compiler_params=pltpu.CompilerParams(
            dimension_semantics=("arbitrary", "arbitrary", "arbitrary"),
            vmem_limit_bytes=VMEM_LIMIT),
        name="expert_ffn",
    )(xg, wg, wu, wd)


def _combine_kernel(idx_ref, val_ref, y_ref, x1_ref, g_ref, nw_ref, o_ref, acc_ref):
    e = pl.program_id(2)
    tb = x1_ref.shape[1]
    n_e, cap, d = y_ref.shape[1:]

    @pl.when(e == 0)
    def _():
        acc_ref[...] = jnp.zeros(acc_ref.shape, F32)

    tok = pl.program_id(1) * tb + lax.broadcasted_iota(jnp.int32, (tb, cap), 0)
    onehot = jnp.concatenate(
        [jnp.where(tok == idx_ref[0, k], val_ref[0, k], 0.0).astype(BF16) for k in range(n_e)],
        axis=1)
    acc_ref[...] += jnp.dot(onehot, y_ref[0].reshape(n_e * cap, d), preferred_element_type=F32)

    @pl.when(e == pl.num_programs(2) - 1)
    def _():
        x = x1_ref[0] + g_ref[0] * acc_ref[...]
        ms = jnp.mean(x * x, axis=-1, keepdims=True)
        o_ref[0] = x * lax.rsqrt(ms + EPS) * nw_ref[...]


def _combine(idx, vals, y, x1, g2, nw, tb, e_blk):
    bsz, L, D = x1.shape
    n_e, cap = y.shape[1], y.shape[2]
    row = pl.BlockSpec((1, tb, D), lambda b, t, e: (b, t, 0))
    return pl.pallas_call(
        _combine_kernel,
        grid=(bsz, L // tb, n_e // e_blk),
        in_specs=[pl.BlockSpec((1, e_blk, 1, cap), lambda b, t, e: (b, e, 0, 0)),
                  pl.BlockSpec((1, e_blk, 1, cap), lambda b, t, e: (b, e, 0, 0)),
                  pl.BlockSpec((1, e_blk, cap, D), lambda b, t, e: (b, e, 0, 0)),
                  row,
                  pl.BlockSpec((1, 1, D), lambda b, t, e: (b, 0, 0)),
                  pl.BlockSpec((1, D), lambda b, t, e: (0, 0))],
        out_specs=row,
        out_shape=jax.ShapeDtypeStruct((bsz, L, D), F32),
        scratch_shapes=[pltpu.VMEM((tb, D), F32)],
        compiler_params=pltpu.CompilerParams(
            dimension_semantics=("parallel", "parallel", "arbitrary"),
            vmem_limit_bytes=VMEM_LIMIT),
        name="combine",
    )(idx, vals, y, x1, g2, nw)


DFT_BATCH = 8


def _dft_kernel(p_ref, q_ref, m1h_ref, m1l_ref, m2h_ref, m2l_ref, tc_ref, ts_ref, o_ref,
                pf_ref, qf_ref, y_ref, of_ref):
    w = GRID_W
    pf_ref[...] = p_ref[0].astype(F32)
    qf_ref[...] = q_ref[0].astype(F32)

    ch = pf_ref.shape[1]
    cols = lambda y, j: y[:, j * ch:(j + 1) * ch]

    def stage1(blk, carry):
        c0 = blk * DFT_BATCH
        v = jnp.concatenate(
            [jnp.concatenate([pf_ref[pl.ds(c0 + j, w, stride=w), :],
                              qf_ref[pl.ds(c0 + j, w, stride=w), :]], axis=0)
             for j in range(DFT_BATCH)], axis=1).astype(BF16)
        y = (jnp.dot(m1h_ref[...], v, preferred_element_type=F32)
             + jnp.dot(m1l_ref[...], v, preferred_element_type=F32))
        for j in range(DFT_BATCH):
            yr, yi = cols(y[:w], j), cols(y[w:], j)
            tc, ts = tc_ref[c0 + j], ts_ref[c0 + j]
            base = pl.multiple_of((c0 + j) * 2 * w, 2 * w)
            y_ref[pl.ds(base, w), :] = yr * tc + yi * ts
            y_ref[pl.ds(base + w, w), :] = yi * tc - yr * ts
        return carry

    lax.fori_loop(0, w // DFT_BATCH, stage1, 0)

    def stage2(blk, carry):
        k0 = blk * DFT_BATCH
        v = jnp.concatenate(
            [jnp.concatenate([y_ref[pl.ds(k0 + j, w, stride=2 * w), :],
                              y_ref[pl.ds(w + k0 + j, w, stride=2 * w), :]], axis=0)
             for j in range(DFT_BATCH)], axis=1).astype(BF16)
        o = (jnp.dot(m2h_ref[...], v, preferred_element_type=F32)
             + jnp.dot(m2l_ref[...], v, preferred_element_type=F32))
        for j in range(DFT_BATCH):
            of_ref[pl.ds(k0 + j, w, stride=w), :] = cols(o, j)
        return carry

    lax.fori_loop(0, w // DFT_BATCH, stage2, 0)
    o_ref[0] = of_ref[...].astype(BF16)


def _split_bf16(m):
    hi = m.astype(BF16)
    return hi, (m - hi.astype(F32)).astype(BF16)


def _fourier_positions(p, q):
    bsz, L, _ = p.shape
    w = GRID_W
    k = jnp.arange(w, dtype=F32)
    ang = (2.0 * jnp.pi / w) * jnp.outer(k, k)
    c, s = jnp.cos(ang), jnp.sin(ang)
    m1h, m1l = _split_bf16(jnp.block([[c, -s], [-s, -c]]))
    m2h, m2l = _split_bf16(jnp.concatenate([c, s], axis=1))
    tw = (2.0 * jnp.pi / L) * jnp.outer(k, k)
    tc = jnp.broadcast_to(jnp.cos(tw)[:, :, None], (w, w, LANE))
    ts = jnp.broadcast_to(jnp.sin(tw)[:, :, None], (w, w, LANE))
    const = lambda a: pl.BlockSpec(a.shape, lambda b, i: (0,) * a.ndim)
    blk = pl.BlockSpec((1, L, LANE), lambda b, i: (b, 0, i))
    return pl.pallas_call(
        _dft_kernel,
        grid=(bsz, F_DIM // LANE),
        in_specs=[blk, blk, const(m1h), const(m1l), const(m2h), const(m2l), const(tc), const(ts)],
        out_specs=blk,
        out_shape=jax.ShapeDtypeStruct((bsz, L, F_DIM), BF16),
        scratch_shapes=[pltpu.VMEM((L, LANE), F32), pltpu.VMEM((L, LANE), F32),
                        pltpu.VMEM((2 * L, LANE), F32), pltpu.VMEM((L, LANE), F32)],
        compiler_params=pltpu.CompilerParams(
            dimension_semantics=("parallel", "parallel"), vmem_limit_bytes=VMEM_LIMIT),
        name="dft_positions",
    )(p, q, m1h, m1l, m2h, m2l, tc, ts)


def _fourier_fold(w_uf, w_fourier):
    n = F_GROUP_DIM
    k = jnp.arange(n, dtype=F32)
    ang = (2.0 * jnp.pi / n) * jnp.outer(k, k)
    hi = lax.Precision.HIGHEST
    scale = 1.0 / jnp.sqrt(jnp.float32(GRID_W * GRID_W * n))
    a = jnp.einsum('cm,gmj->gcj', jnp.cos(ang), w_fourier, precision=hi) * scale
    b = jnp.einsum('cm,gmj->gcj', jnp.sin(ang), w_fourier, precision=hi) * scale
    wg = w_uf.reshape(-1, F_GROUPS, n)
    wp = jnp.einsum('dgc,gcj->dgj', wg, a, precision=hi).reshape(-1, F_DIM)
    wq = jnp.einsum('dgc,gcj->dgj', wg, b, precision=hi).reshape(-1, F_DIM)
    return wp, wq


def kernel(x, c, ctx, c_ctx, w_ada, b_ada, norm1_w, w_in, conv_w, conv_b, dt_bias, a_log,
           d_skip, ssm_norm_w, w_fourier, w_out, norm2_w, w_router, w_gate, w_up, w_down,
           final_norm_w):
    bsz, L, D = x.shape
    hi = lax.Precision.HIGHEST
    mod_l = (jnp.dot(jax.nn.silu(c), w_ada[0], precision=hi) + b_ada[0]).reshape(bsz, N_MOD, 1, D)
    mod_c = (jnp.dot(jax.nn.silu(c_ctx), w_ada[0], precision=hi) + b_ada[0]).reshape(N_MOD, 1, 1, D)
    sh1, sc1, g1, sh2, sc2, g2 = [mod_l[:, i] for i in range(N_MOD)]
    csh1, csc1 = mod_c[0], mod_c[1]

    dt_pad = DT_PAD - 2 * SSM_HEADS
    assert L == GRID_W * GRID_W
    w_p, w_q = _fourier_fold(w_in[0][:, :F_DIM], w_fourier[0])
    w_in_p = jnp.concatenate(
        [w_p, w_q, w_in[0][:, F_DIM:], jnp.zeros((D, dt_pad), F32)], axis=1).astype(BF16)
    dtb_row = jnp.pad(dt_bias[0].reshape(1, 2 * SSM_HEADS), ((0, 0), (0, dt_pad)))
    a_row = jnp.pad(-jnp.exp(a_log[0]).reshape(1, 2 * SSM_HEADS), ((0, 0), (0, dt_pad)))
    nw1 = norm1_w[0][None]
    conv_b_row = conv_b[0][None]

    xs_c, bc_c, dt_c = _inproj(ctx, csh1, csc1, nw1, w_in_p, conv_w[0], conv_b_row, dtb_row,
                               emit_fz=False, tm=ctx.shape[1])
    p, q, z, xs, bc, dt = _inproj(x, sh1, sc1, nw1, w_in_p, conv_w[0], conv_b_row, dtb_row,
                                  emit_fz=True, tm=512)
    y_fwd = _ssd(xs, bc, dt, xs_c, bc_c, dt_c, a_row, False, None, rows_per_step=512, n_seq=2)
    dsk_row = jnp.repeat(d_skip[0], SSM_HEAD_DIM)[None]
    y_ssm = _ssd(xs, bc, dt, xs_c, bc_c, dt_c, a_row, True,
                 (y_fwd, z, dsk_row, ssm_norm_w[0][None]), rows_per_step=512, n_seq=2)
    four = _fourier_positions(p, q)

    w_r = jnp.pad(w_router[0], ((0, 0), (0, LANE - N_EXPERTS)))
    x1, h2, probs = _outproj(four, y_ssm, x, g1, sh2, sc2, norm2_w[0][None],
                             w_out[0].astype(BF16), w_r, tm=512)

    cap = CAPACITY_FACTOR * L // N_EXPERTS
    vals, idx = lax.top_k(jnp.swapaxes(probs[..., :N_EXPERTS], 1, 2), cap)
    bidx = jnp.arange(bsz)[:, None, None]
    xg = h2[bidx, idx]
    y = _expert_ffn(xg, w_gate[0], w_up[0], w_down[0])
    return _combine(idx[:, :, None, :], vals[:, :, None, :], y, x1, g2, final_norm_w[None],
                    tb=1024, e_blk=8)
```

```python
import functools

import jax
import jax.numpy as jnp
from jax import lax
from jax.experimental import pallas as pl
from jax.experimental.pallas import tpu as pltpu

D_MODEL = 1024
GRID_W = 64
F_GROUP_DIM = 64
F_DIM = D_MODEL // 2
F_GROUPS = F_DIM // F_GROUP_DIM
SSM_HEAD_DIM = 64
D_SSM = 3 * D_MODEL // 2
SSM_HEADS = D_SSM // SSM_HEAD_DIM
SSM_GROUPS = 4
D_STATE = 128
CONV_K = 5
CHUNK = 128
D_MIX = F_DIM + D_SSM
CONV_DIM = D_SSM + 2 * SSM_GROUPS * D_STATE
D_IN_PROJ = F_DIM + D_SSM + CONV_DIM + 2 * SSM_HEADS
N_EXPERTS = 16
CAPACITY_FACTOR = 2
D_EXPERT = 2048
N_MOD = 6
EPS = 1e-6

LANE = 128
DT_PAD = LANE
VMEM_LIMIT = 56 * 1024 * 1024

F32 = jnp.float32
BF16 = jnp.bfloat16


def _norm_mod(x, nw, shift, scale):
    ms = jnp.mean(x * x, axis=-1, keepdims=True)
    y = x * lax.rsqrt(ms + EPS) * nw
    return y * (1.0 + scale) + shift


HALO = 8
assert CONV_K == 5
Z0 = 2 * F_DIM
XBC0 = Z0 + D_SSM
DT0 = XBC0 + CONV_DIM
CONV_COLS = 512
CONV_ROWS = 128
MM_COLS = 256


def _softplus(v):
    return jnp.maximum(v, 0.0) + jnp.log1p(jnp.exp(-jnp.abs(v)))


def _inproj_kernel(x_ref, xp_ref, xn_ref, sh_ref, sc_ref, nw_ref, w_ref, cw_ref, cb_ref, dtb_ref,
                   *refs, emit_fz):
    n_conv = CONV_DIM // CONV_COLS
    s_refs = refs[-n_conv:]
    if emit_fz:
        p_ref, q_ref, z_ref, xs_ref, bc_ref, dt_ref = refs[:-n_conv]
    else:
        xs_ref, bc_ref, dt_ref = refs[:-n_conv]
    i = pl.program_id(1)
    tm = x_ref.shape[1]
    nw, sh, sc = nw_ref[...], sh_ref[0], sc_ref[0]
    h = _norm_mod(x_ref[0], nw, sh, sc).astype(BF16)
    halo = jnp.concatenate([xp_ref[0], xn_ref[0]], axis=0)
    hh = _norm_mod(halo, nw, sh, sc).astype(BF16)
    dt_raw = jnp.dot(h, w_ref[:, DT0:DT0 + DT_PAD], preferred_element_type=F32)
    dt_ref[0] = _softplus(dt_raw + dtb_ref[...])

    def preconv(ci, half):
        c = half * MM_COLS
        wc = w_ref[:, XBC0 + ci * CONV_COLS + c:XBC0 + ci * CONV_COLS + c + MM_COLS]
        s_ref = s_refs[ci]
        s_ref[HALO:HALO + tm, c:c + MM_COLS] = jnp.dot(h, wc, preferred_element_type=F32)
        hal = jnp.dot(hh, wc, preferred_element_type=F32)
        s_ref[0:HALO, c:c + MM_COLS] = jnp.where(i > 0, hal[:HALO], 0.0)
        s_ref[HALO + tm:, c:c + MM_COLS] = jnp.where(
            i < pl.num_programs(1) - 1, hal[HALO:], 0.0)

    def plain(o_ref, oc, wc0):
        o_ref[0, :, oc:oc + MM_COLS] = jnp.dot(
            h, w_ref[:, wc0:wc0 + MM_COLS], preferred_element_type=F32).astype(BF16)

    def conv(ci, r0):
        c0 = ci * CONV_COLS
        n = CONV_ROWS + 2 * HALO
        xw = s_refs[ci][pl.ds(r0, n), :]
        t = [cw_ref[k:k + 1, c0:c0 + CONV_COLS] * xw for k in range(CONV_K)]
        up = lambda v: pltpu.roll(v, n - 1, 0)
        down = lambda v: pltpu.roll(v, 1, 0)
        y = t[2] + up(t[3] + up(t[4])) + down(t[1] + down(t[0]))
        acc = y[HALO:HALO + CONV_ROWS] + cb_ref[:, c0:c0 + CONV_COLS]
        act = (acc * jax.nn.sigmoid(acc)).astype(BF16)
        if c0 < D_SSM:
            xs_ref[0, r0:r0 + CONV_ROWS, c0:c0 + CONV_COLS] = act
        else:
            bc_ref[0, r0:r0 + CONV_ROWS, c0 - D_SSM:c0 - D_SSM + CONV_COLS] = act

    n_conv = CONV_DIM // CONV_COLS
    halves = CONV_COLS // MM_COLS
    mm = []
    for ci in range(n_conv):
        if ci + 1 < n_conv:
            mm += [functools.partial(preconv, ci + 1, hf) for hf in range(halves)]
    if emit_fz:
        extra = ([(p_ref, c, c) for c in range(0, F_DIM, MM_COLS)]
                 + [(q_ref, c, F_DIM + c) for c in range(0, F_DIM, MM_COLS)]
                 + [(z_ref, c, Z0 + c) for c in range(0, D_SSM, MM_COLS)])
        per = -(-len(extra) // n_conv)
        merged = []
        for ci in range(n_conv):
            merged += mm[ci * halves:(ci + 1) * halves]
            merged += [functools.partial(plain, *e) for e in extra[ci * per:(ci + 1) * per]]
        mm = merged
    for hf in range(halves):
        preconv(0, hf)
    blocks = [(ci, r0) for ci in range(n_conv) for r0 in range(0, tm, CONV_ROWS)]
    per_block = -(-len(mm) // len(blocks))
    for bi, (ci, r0) in enumerate(blocks):
        for piece in mm[bi * per_block:(bi + 1) * per_block]:
            piece()
        conv(ci, r0)


def _inproj(x, shift, scale, nw, w, conv_w, conv_b, dt_bias, emit_fz, tm):
    bsz, L, D = x.shape
    n_mod = shift.shape[0]
    mod_map = (lambda b, i: (b, 0, 0)) if n_mod > 1 else (lambda b, i: (0, 0, 0))
    const = lambda b, i: (0, 0)
    row = lambda n: pl.BlockSpec((1, tm, n), lambda b, i: (b, i, 0))
    hb = tm // HALO
    widths = ([F_DIM, F_DIM, D_SSM] if emit_fz else []) + [D_SSM, 2 * SSM_GROUPS * D_STATE, DT_PAD]
    dtypes = ([BF16, BF16, BF16] if emit_fz else []) + [BF16, BF16, F32]
    return pl.pallas_call(
        functools.partial(_inproj_kernel, emit_fz=emit_fz),
        grid=(bsz, L // tm),
        in_specs=[
            row(D),
            pl.BlockSpec((1, HALO, D), lambda b, i: (b, jnp.maximum(i * hb - 1, 0), 0)),
            pl.BlockSpec((1, HALO, D), lambda b, i: (b, jnp.minimum((i + 1) * hb, L // HALO - 1), 0)),
            pl.BlockSpec((1, 1, D), mod_map),
            pl.BlockSpec((1, 1, D), mod_map),
            pl.BlockSpec((1, D), const),
            pl.BlockSpec(w.shape, const),
            pl.BlockSpec(conv_w.shape, const),
            pl.BlockSpec(conv_b.shape, const),
            pl.BlockSpec(dt_bias.shape, const),
        ],
        out_specs=[row(n) for n in widths],
        out_shape=[jax.ShapeDtypeStruct((bsz, L, n), dt) for n, dt in zip(widths, dtypes)],
        scratch_shapes=[pltpu.VMEM((tm + 2 * HALO, CONV_COLS), F32)] * (CONV_DIM // CONV_COLS),
        compiler_params=pltpu.CompilerParams(
            dimension_semantics=("parallel", "arbitrary"),
            vmem_limit_bytes=VMEM_LIMIT),
        name="inproj",
    )(x, x, x, shift, scale, nw, w, conv_w, conv_b, dt_bias)


HEADS_PER_GROUP = SSM_HEADS // SSM_GROUPS
GROUP_W = HEADS_PER_GROUP * SSM_HEAD_DIM
PAIRS_PER_GROUP = GROUP_W // LANE


def _ssd_chunk(ldx, ldbc, dt, a_row, ex_ref, st_ref, rt_ref, reverse, emit):
    off = SSM_HEADS if reverse else 0
    end = 0 if reverse else CHUNK - 1
    r = lax.broadcasted_iota(jnp.int32, (CHUNK, CHUNK), 0)
    s = lax.broadcasted_iota(jnp.int32, (CHUNK, CHUNK), 1)
    keep = (r <= s) if reverse else (r >= s)
    left = lax.broadcasted_iota(jnp.int32, (CHUNK, LANE), 1) < SSM_HEAD_DIM
    left_bf = left.astype(BF16)
    right_bf = 1 - left_bf

    def expand_row(v, j):
        return jnp.where(left[:1], v[:, j:j + 1], v[:, j + 1:j + 2])

    dA = dt * a_row
    tri = keep.astype(BF16)
    p1 = dA.astype(BF16)
    r1 = dA - p1.astype(F32)
    p2 = r1.astype(BF16)
    p3 = (r1 - p2.astype(F32)).astype(BF16)
    acs = (jnp.dot(tri, p1, preferred_element_type=F32)
           + jnp.dot(tri, p2, preferred_element_type=F32)
           + jnp.dot(tri, p3, preferred_element_type=F32))
    yield
    acs_end = acs[end:end + 1, :]
    w_state = jnp.exp(acs_end - acs) * dt
    dec_row = jnp.exp(acs_end)
    if emit is not None:
        rt_ref[...] = (jnp.log(dt) - acs).T
        e_in = jnp.exp(acs)
    def expand(v, j):
        return jnp.take_along_axis(v, jnp.where(left, j, j + 1), axis=1,
                                   mode="promise_in_bounds")

    w_wide = jnp.dot(w_state.astype(BF16), ex_ref[...], preferred_element_type=F32)
    yield

    for g in range(SSM_GROUPS):
        bg = ldbc(D_STATE * g, D_STATE)
        cg = ldbc(SSM_GROUPS * D_STATE + D_STATE * g, D_STATE)
        hst = st_ref[g]
        if emit is not None:
            cb = lax.dot_general(cg, bg, (((1,), (1,)), ((), ())), preferred_element_type=F32)
            y_off = jnp.dot(cg, hst.astype(BF16), preferred_element_type=F32)
        xw, dec, ys = [], [], []
        for p in range(PAIRS_PER_GROUP):
            j = off + HEADS_PER_GROUP * g + 2 * p
            c0 = GROUP_W * g + LANE * p
            xp = ldx(c0, LANE)
            xw.append((xp.astype(F32) * w_wide[:, c0:c0 + LANE]).astype(BF16))
            dec.append(expand_row(dec_row, j))
            if emit is not None:
                ms = []
                for jj in (j, j + 1):
                    seg = acs[:, jj:jj + 1] + rt_ref[jj:jj + 1, :]
                    ms.append((cb * jnp.exp(jnp.where(keep, seg, -jnp.inf))).astype(BF16))
                lhs = jnp.concatenate(ms, axis=1)
                rhs = jnp.concatenate([xp * left_bf, xp * right_bf], axis=0)
                y_d = jnp.dot(lhs, rhs, preferred_element_type=F32)
                ys.append(y_d + y_off[:, LANE * p:LANE * (p + 1)] * expand(e_in, j))
            yield
        upd = lax.dot_general(bg, jnp.concatenate(xw, axis=1), (((0,), (0,)), ((), ())),
                              preferred_element_type=F32)
        st_ref[g] = hst * jnp.concatenate(dec, axis=1) + upd
        if emit is not None:
            emit(g, jnp.concatenate(ys, axis=1))
        yield


def _interleave(gens):
    gens = list(gens)
    while gens:
        for gen in list(gens):
            try:
                next(gen)
            except StopIteration:
                gens.remove(gen)


def _ssd_kernel(*refs, reverse, final):
    if final:
        (xs_ref, bc_ref, dt_ref, xsc_ref, bcc_ref, dtc_ref, ar_ref, ex_ref,
         yf_ref, z_ref, dsk_ref, gw_ref, o_ref, st_ref, rt_ref) = refs
    else:
        (xs_ref, bc_ref, dt_ref, xsc_ref, bcc_ref, dtc_ref, ar_ref, ex_ref,
         o_ref, st_ref, rt_ref) = refs
    n_seq = xs_ref.shape[0]
    n_chunks = xs_ref.shape[1] // CHUNK
    n_ctx = xsc_ref.shape[1] // CHUNK
    a_row = ar_ref[...]

    @pl.when(pl.program_id(1) == 0)
    def _():
        st_ref[...] = jnp.zeros(st_ref.shape, F32)
        for c in (range(n_ctx - 1, -1, -1) if reverse else range(n_ctx)):
            rows = pl.ds(c * CHUNK, CHUNK)
            _interleave(
                _ssd_chunk(lambda c0, w, n=n: xsc_ref[n, rows, c0:c0 + w],
                           lambda c0, w, n=n: bcc_ref[n, rows, c0:c0 + w],
                           dtc_ref[n, rows, :], a_row, ex_ref, st_ref.at[n], rt_ref.at[n],
                           reverse, None)
                for n in range(n_seq))

    def body(k, carry):
        c = (n_chunks - 1 - k) if reverse else k
        rows = pl.ds(pl.multiple_of(c * CHUNK, CHUNK), CHUNK)

        def emit(n, g, y):
            cols = slice(GROUP_W * g, GROUP_W * (g + 1))
            if final:
                y = y + yf_ref[n, rows, cols].astype(F32)
                y = y + dsk_ref[:, cols] * xs_ref[n, rows, cols].astype(F32)
                z = z_ref[n, rows, cols].astype(F32)
                y = y * (z * jax.nn.sigmoid(z))
                y = y * lax.rsqrt(jnp.mean(y * y, axis=-1, keepdims=True) + EPS) * gw_ref[:, cols]
            o_ref[n, rows, cols] = y.astype(BF16)

        _interleave(
            _ssd_chunk(lambda c0, w, n=n: xs_ref[n, rows, c0:c0 + w],
                       lambda c0, w, n=n: bc_ref[n, rows, c0:c0 + w],
                       dt_ref[n, rows, :], a_row, ex_ref, st_ref.at[n], rt_ref.at[n],
                       reverse, functools.partial(emit, n))
            for n in range(n_seq))
        return carry

    lax.fori_loop(0, n_chunks, body, 0)


def _ssd(xs, bc, dt, xs_c, bc_c, dt_c, a_row, reverse, extra, rows_per_step, n_seq):
    bsz, L, _ = xs.shape
    n_blk = L // rows_per_step
    blk = (lambda b, i: (b, n_blk - 1 - i, 0)) if reverse else (lambda b, i: (b, i, 0))
    row = lambda n: pl.BlockSpec((n_seq, rows_per_step, n), blk)
    ctx = lambda a: pl.BlockSpec((n_seq,) + a.shape[1:], lambda b, i: (b, 0, 0))
    const = lambda a: pl.BlockSpec(a.shape, lambda b, i: (0, 0))
    head_of_lane = (SSM_HEADS if reverse else 0) + jnp.arange(D_SSM) // SSM_HEAD_DIM
    ex = (jnp.arange(DT_PAD)[:, None] == head_of_lane[None, :]).astype(BF16)
    in_specs = [row(D_SSM), row(bc.shape[2]), row(DT_PAD), ctx(xs_c), ctx(bc_c), ctx(dt_c),
                const(a_row), const(ex)]
    args = [xs, bc, dt, xs_c, bc_c, dt_c, a_row, ex]
    if extra is not None:
        y_fwd, z, dsk, gw = extra
        in_specs += [row(D_SSM), row(D_SSM), const(dsk), const(gw)]
        args += [y_fwd, z, dsk, gw]
    return pl.pallas_call(
        functools.partial(_ssd_kernel, reverse=reverse, final=extra is not None),
        grid=(bsz // n_seq, n_blk),
        in_specs=in_specs,
        out_specs=row(D_SSM),
        out_shape=jax.ShapeDtypeStruct((bsz, L, D_SSM), BF16),
        scratch_shapes=[pltpu.VMEM((n_seq, SSM_GROUPS, D_STATE, GROUP_W), F32),
                        pltpu.VMEM((n_seq, DT_PAD, CHUNK), F32)],
        compiler_params=pltpu.CompilerParams(
            dimension_semantics=("parallel", "arbitrary"),
            vmem_limit_bytes=VMEM_LIMIT),
        name="ssd_bwd" if reverse else "ssd_fwd",
    )(*args)


def _outproj_kernel(f_ref, y_ref, x_ref, g1_ref, sh_ref, sc_ref, nw_ref, wo_ref, wr_ref,
                    x1_ref, h2_ref, p_ref):
    m = jnp.dot(f_ref[0], wo_ref[:F_DIM, :], preferred_element_type=F32)
    m = m + jnp.dot(y_ref[0], wo_ref[F_DIM:, :], preferred_element_type=F32)
    x1 = x_ref[0] + g1_ref[0] * m
    x1_ref[0] = x1
    h2 = _norm_mod(x1, nw_ref[...], sh_ref[0], sc_ref[0])
    h2_ref[0] = h2.astype(BF16)
    logits = jnp.dot(h2, wr_ref[...], preferred_element_type=F32)
    lane = lax.broadcasted_iota(jnp.int32, logits.shape, 1)
    logits = jnp.where(lane < N_EXPERTS, logits, -jnp.inf)
    e = jnp.exp(logits - jnp.max(logits, axis=-1, keepdims=True))
    p_ref[0] = e / jnp.sum(e, axis=-1, keepdims=True)


def _outproj(four, yssm, x, g1, sh2, sc2, nw, wo, wr, tm):
    bsz, L, D = x.shape
    row = lambda n: pl.BlockSpec((1, tm, n), lambda b, i: (b, i, 0))
    mod = pl.BlockSpec((1, 1, D), lambda b, i: (b, 0, 0))
    return pl.pallas_call(
        _outproj_kernel,
        grid=(bsz, L // tm),
        in_specs=[row(F_DIM), row(D_SSM), row(D), mod, mod, mod,
                  pl.BlockSpec((1, D), lambda b, i: (0, 0)),
                  pl.BlockSpec(wo.shape, lambda b, i: (0, 0)),
                  pl.BlockSpec(wr.shape, lambda b, i: (0, 0))],
        out_specs=[row(D), row(D), row(LANE)],
        out_shape=[jax.ShapeDtypeStruct((bsz, L, D), F32),
                   jax.ShapeDtypeStruct((bsz, L, D), BF16),
                   jax.ShapeDtypeStruct((bsz, L, LANE), F32)],
        compiler_params=pltpu.CompilerParams(
            dimension_semantics=("parallel", "parallel"),
            vmem_limit_bytes=VMEM_LIMIT),
        name="outproj",
    )(four, yssm, x, g1, sh2, sc2, nw, wo, wr)


FFN_SLABS = 2
FFN_SLAB = D_EXPERT // FFN_SLABS
FFN_COLS = 512


def _ffn_kernel(x_ref, wg_ref, wu_ref, wd_ref, o_ref, wgb_ref, wub_ref, wdb_ref, acc_ref):
    b, f = pl.program_id(1), pl.program_id(2)

    @pl.when(b == 0)
    def _():
        wgb_ref[f] = wg_ref[0].astype(BF16)
        wub_ref[f] = wu_ref[0].astype(BF16)
        wdb_ref[f] = wd_ref[0].astype(BF16)

    x = x_ref[0, 0]
    part = jnp.zeros(acc_ref.shape, F32)
    for c in range(0, FFN_SLAB, FFN_COLS):
        g = jnp.dot(x, wgb_ref[f, :, c:c + FFN_COLS], preferred_element_type=F32)
        u = jnp.dot(x, wub_ref[f, :, c:c + FFN_COLS], preferred_element_type=F32)
        a = (g * jax.nn.sigmoid(g) * u).astype(BF16)
        part = part + jnp.dot(a, wdb_ref[f, c:c + FFN_COLS, :], preferred_element_type=F32)

    @pl.when(f == 0)
    def _():
        acc_ref[...] = part

    @pl.when(f > 0)
    def _():
        acc_ref[...] += part

    @pl.when(f == FFN_SLABS - 1)
    def _():
        o_ref[0, 0] = acc_ref[...].astype(o_ref.dtype)


def _expert_ffn(xg, wg, wu, wd):
    bsz, n_e, cap, D = xg.shape
    slab = lambda b, f: jnp.where(b == 0, f, FFN_SLABS - 1)
    return pl.pallas_call(
        _ffn_kernel,
        grid=(n_e, bsz, FFN_SLABS),
        in_specs=[
            pl.BlockSpec((1, 1, cap, D), lambda e, b, f: (b, e, 0, 0)),
            pl.BlockSpec((1, D, FFN_SLAB), lambda e, b, f: (e, 0, slab(b, f))),
            pl.BlockSpec((1, D, FFN_SLAB), lambda e, b, f: (e, 0, slab(b, f))),
            pl.BlockSpec((1, FFN_SLAB, D), lambda e, b, f: (e, slab(b, f), 0)),
        ],
        out_specs=pl.BlockSpec((1, 1, cap, D), lambda e, b, f: (b, e, 0, 0)),
        out_shape=jax.ShapeDtypeStruct((bsz, n_e, cap, D), BF16),
        scratch_shapes=[pltpu.VMEM((FFN_SLABS, D, FFN_SLAB), BF16),
                        pltpu.VMEM((FFN_SLABS, D, FFN_SLAB), BF16),
                        pltpu.VMEM((FFN_SLABS, FFN_SLAB, D), BF16),
                        pltpu.VMEM((cap, D), F32)],
        compiler_params=pltpu.CompilerParams(
            dimension_semantics=("arbitrary", "arbitrary", "arbitrary"),
            vmem_limit_bytes=VMEM_LIMIT),
        name="expert_ffn",
    )(xg, wg, wu, wd)


def _combine_kernel(idx_ref, val_ref, y_ref, x1_ref, g_ref, nw_ref, o_ref, acc_ref):
    e = pl.program_id(2)
    tb = x1_ref.shape[1]
    n_e, cap, d = y_ref.shape[1:]

    @pl.when(e == 0)
    def _():
        acc_ref[...] = jnp.zeros(acc_ref.shape, F32)

    tok = pl.program_id(1) * tb + lax.broadcasted_iota(jnp.int32, (tb, cap), 0)
    onehot = jnp.concatenate(
        [jnp.where(tok == idx_ref[0, k], val_ref[0, k], 0.0).astype(BF16) for k in range(n_e)],
        axis=1)
    acc_ref[...] += jnp.dot(onehot, y_ref[0].reshape(n_e * cap, d), preferred_element_type=F32)

    @pl.when(e == pl.num_programs(2) - 1)
    def _():
        x = x1_ref[0] + g_ref[0] * acc_ref[...]
        ms = jnp.mean(x * x, axis=-1, keepdims=True)
        o_ref[0] = x * lax.rsqrt(ms + EPS) * nw_ref[...]


def _combine(idx, vals, y, x1, g2, nw, tb, e_blk):
    bsz, L, D = x1.shape
    n_e, cap = y.shape[1], y.shape[2]
    row = pl.BlockSpec((1, tb, D), lambda b, t, e: (b, t, 0))
    return pl.pallas_call(
        _combine_kernel,
        grid=(bsz, L // tb, n_e // e_blk),
        in_specs=[pl.BlockSpec((1, e_blk, 1, cap), lambda b, t, e: (b, e, 0, 0)),
                  pl.BlockSpec((1, e_blk, 1, cap), lambda b, t, e: (b, e, 0, 0)),
                  pl.BlockSpec((1, e_blk, cap, D), lambda b, t, e: (b, e, 0, 0)),
                  row,
                  pl.BlockSpec((1, 1, D), lambda b, t, e: (b, 0, 0)),
                  pl.BlockSpec((1, D), lambda b, t, e: (0, 0))],
        out_specs=row,
        out_shape=jax.ShapeDtypeStruct((bsz, L, D), F32),
        scratch_shapes=[pltpu.VMEM((tb, D), F32)],
        compiler_params=pltpu.CompilerParams(
            dimension_semantics=("parallel", "parallel", "arbitrary"),
            vmem_limit_bytes=VMEM_LIMIT),
        name="combine",
    )(idx, vals, y, x1, g2, nw)


DFT_BATCH = 16
DFT_PAD = 8


def _dft_kernel(p_ref, q_ref, m1h_ref, m1l_ref, m2h_ref, m2l_ref, tc_ref, ts_ref, o_ref,
                pf_ref, qf_ref, y_ref, of_ref):
    w = GRID_W
    pitch, ypitch = w + DFT_PAD, 2 * w + DFT_PAD
    for r in range(w):
        pf_ref[r * pitch:r * pitch + w, :] = p_ref[0, r * w:(r + 1) * w, :].astype(F32)
        qf_ref[r * pitch:r * pitch + w, :] = q_ref[0, r * w:(r + 1) * w, :].astype(F32)

    ch = pf_ref.shape[1]
    cols = lambda y, j: y[:, j * ch:(j + 1) * ch]

    def stage1(blk, carry):
        c0 = blk * DFT_BATCH
        v = jnp.concatenate(
            [jnp.concatenate([pf_ref[pl.ds(c0 + j, w, stride=pitch), :],
                              qf_ref[pl.ds(c0 + j, w, stride=pitch), :]], axis=0)
             for j in range(DFT_BATCH)], axis=1).astype(BF16)
        y = (jnp.dot(m1h_ref[...], v, preferred_element_type=F32)
             + jnp.dot(m1l_ref[...], v, preferred_element_type=F32))
        for j in range(DFT_BATCH):
            yr, yi = cols(y[:w], j), cols(y[w:], j)
            tc, ts = tc_ref[c0 + j], ts_ref[c0 + j]
            base = pl.multiple_of((c0 + j) * ypitch, DFT_PAD)
            y_ref[pl.ds(base, w), :] = yr * tc + yi * ts
            y_ref[pl.ds(base + w, w), :] = yi * tc - yr * ts
        return carry

    lax.fori_loop(0, w // DFT_BATCH, stage1, 0)

    def stage2(blk, carry):
        k0 = blk * DFT_BATCH
        v = jnp.concatenate(
            [jnp.concatenate([y_ref[pl.ds(k0 + j, w, stride=ypitch), :],
                              y_ref[pl.ds(w + k0 + j, w, stride=ypitch), :]], axis=0)
             for j in range(DFT_BATCH)], axis=1).astype(BF16)
        o = (jnp.dot(m2h_ref[...], v, preferred_element_type=F32)
             + jnp.dot(m2l_ref[...], v, preferred_element_type=F32))
        for j in range(DFT_BATCH):
            of_ref[pl.ds(k0 + j, w, stride=pitch), :] = cols(o, j)
        return carry

    lax.fori_loop(0, w // DFT_BATCH, stage2, 0)
    for k2 in range(w):
        o_ref[0, k2 * w:(k2 + 1) * w, :] = of_ref[k2 * pitch:k2 * pitch + w, :].astype(BF16)


def _split_bf16(m):
    hi = m.astype(BF16)
    return hi, (m - hi.astype(F32)).astype(BF16)


def _fourier_positions(p, q):
    bsz, L, _ = p.shape
    w = GRID_W
    k = jnp.arange(w, dtype=F32)
    ang = (2.0 * jnp.pi / w) * jnp.outer(k, k)
    c, s = jnp.cos(ang), jnp.sin(ang)
    m1h, m1l = _split_bf16(jnp.block([[c, -s], [-s, -c]]))
    m2h, m2l = _split_bf16(jnp.concatenate([c, s], axis=1))
    tw = (2.0 * jnp.pi / L) * jnp.outer(k, k)
    tc = jnp.broadcast_to(jnp.cos(tw)[:, :, None], (w, w, LANE))
    ts = jnp.broadcast_to(jnp.sin(tw)[:, :, None], (w, w, LANE))
    const = lambda a: pl.BlockSpec(a.shape, lambda b, i: (0,) * a.ndim)
    blk = pl.BlockSpec((1, L, LANE), lambda b, i: (b, 0, i))
    return pl.pallas_call(
        _dft_kernel,
        grid=(bsz, F_DIM // LANE),
        in_specs=[blk, blk, const(m1h), const(m1l), const(m2h), const(m2l), const(tc), const(ts)],
        out_specs=blk,
        out_shape=jax.ShapeDtypeStruct((bsz, L, F_DIM), BF16),
        scratch_shapes=[pltpu.VMEM((w * (w + DFT_PAD), LANE), F32),
                        pltpu.VMEM((w * (w + DFT_PAD), LANE), F32),
                        pltpu.VMEM((w * (2 * w + DFT_PAD), LANE), F32),
                        pltpu.VMEM((w * (w + DFT_PAD), LANE), F32)],
        compiler_params=pltpu.CompilerParams(
            dimension_semantics=("parallel", "parallel"), vmem_limit_bytes=VMEM_LIMIT),
        name="dft_positions",
    )(p, q, m1h, m1l, m2h, m2l, tc, ts)


def _fourier_fold(w_uf, w_fourier):
    n = F_GROUP_DIM
    k = jnp.arange(n, dtype=F32)
    ang = (2.0 * jnp.pi / n) * jnp.outer(k, k)
    hi = lax.Precision.HIGHEST
    scale = 1.0 / jnp.sqrt(jnp.float32(GRID_W * GRID_W * n))
    a = jnp.einsum('cm,gmj->gcj', jnp.cos(ang), w_fourier, precision=hi) * scale
    b = jnp.einsum('cm,gmj->gcj', jnp.sin(ang), w_fourier, precision=hi) * scale
    wg = w_uf.reshape(-1, F_GROUPS, n)
    wp = jnp.einsum('dgc,gcj->dgj', wg, a, precision=hi).reshape(-1, F_DIM)
    wq = jnp.einsum('dgc,gcj->dgj', wg, b, precision=hi).reshape(-1, F_DIM)
    return wp, wq


def kernel(x, c, ctx, c_ctx, w_ada, b_ada, norm1_w, w_in, conv_w, conv_b, dt_bias, a_log,
           d_skip, ssm_norm_w, w_fourier, w_out, norm2_w, w_router, w_gate, w_up, w_down,
           final_norm_w):
    bsz, L, D = x.shape
    hi = lax.Precision.HIGHEST
    mod_l = (jnp.dot(jax.nn.silu(c), w_ada[0], precision=hi) + b_ada[0]).reshape(bsz, N_MOD, 1, D)
    mod_c = (jnp.dot(jax.nn.silu(c_ctx), w_ada[0], precision=hi) + b_ada[0]).reshape(N_MOD, 1, 1, D)
    sh1, sc1, g1, sh2, sc2, g2 = [mod_l[:, i] for i in range(N_MOD)]
    csh1, csc1 = mod_c[0], mod_c[1]

    dt_pad = DT_PAD - 2 * SSM_HEADS
    assert L == GRID_W * GRID_W
    w_p, w_q = _fourier_fold(w_in[0][:, :F_DIM], w_fourier[0])
    w_in_p = jnp.concatenate(
        [w_p, w_q, w_in[0][:, F_DIM:], jnp.zeros((D, dt_pad), F32)], axis=1).astype(BF16)
    dtb_row = jnp.pad(dt_bias[0].reshape(1, 2 * SSM_HEADS), ((0, 0), (0, dt_pad)))
    a_row = jnp.pad(-jnp.exp(a_log[0]).reshape(1, 2 * SSM_HEADS), ((0, 0), (0, dt_pad)))
    nw1 = norm1_w[0][None]
    conv_b_row = conv_b[0][None]

    xs_c, bc_c, dt_c = _inproj(ctx, csh1, csc1, nw1, w_in_p, conv_w[0], conv_b_row, dtb_row,
                               emit_fz=False, tm=ctx.shape[1])
    p, q, z, xs, bc, dt = _inproj(x, sh1, sc1, nw1, w_in_p, conv_w[0], conv_b_row, dtb_row,
                                  emit_fz=True, tm=512)
    y_fwd = _ssd(xs, bc, dt, xs_c, bc_c, dt_c, a_row, False, None, rows_per_step=512, n_seq=2)
    dsk_row = jnp.repeat(d_skip[0], SSM_HEAD_DIM)[None]
    y_ssm = _ssd(xs, bc, dt, xs_c, bc_c, dt_c, a_row, True,
                 (y_fwd, z, dsk_row, ssm_norm_w[0][None]), rows_per_step=512, n_seq=2)
    four = _fourier_positions(p, q)

    w_r = jnp.pad(w_router[0], ((0, 0), (0, LANE - N_EXPERTS)))
    x1, h2, probs = _outproj(four, y_ssm, x, g1, sh2, sc2, norm2_w[0][None],
                             w_out[0].astype(BF16), w_r, tm=512)

    cap = CAPACITY_FACTOR * L // N_EXPERTS
    vals, idx = lax.top_k(jnp.swapaxes(probs[..., :N_EXPERTS], 1, 2), cap)
    bidx = jnp.arange(bsz)[:, None, None]
    xg = h2[bidx, idx]
    y = _expert_ffn(xg, w_gate[0], w_up[0], w_down[0])
    return _combine(idx[:, :, None, :], vals[:, :, None, :], y, x1, g2, final_norm_w[None],
                    tb=1024, e_blk=8)
```

```python
import functools

import jax
import jax.numpy as jnp
from jax import lax
from jax.experimental import pallas as pl
from jax.experimental.pallas import tpu as pltpu

D_MODEL = 1024
GRID_W = 64
F_GROUP_DIM = 64
F_DIM = D_MODEL // 2
F_GROUPS = F_DIM // F_GROUP_DIM
SSM_HEAD_DIM = 64
D_SSM = 3 * D_MODEL // 2
SSM_HEADS = D_SSM // SSM_HEAD_DIM
SSM_GROUPS = 4
D_STATE = 128
CONV_K = 5
CHUNK = 128
D_MIX = F_DIM + D_SSM
CONV_DIM = D_SSM + 2 * SSM_GROUPS * D_STATE
D_IN_PROJ = F_DIM + D_SSM + CONV_DIM + 2 * SSM_HEADS
N_EXPERTS = 16
CAPACITY_FACTOR = 2
D_EXPERT = 2048
N_MOD = 6
EPS = 1e-6

LANE = 128
DT_PAD = LANE
VMEM_LIMIT = 56 * 1024 * 1024

F32 = jnp.float32
BF16 = jnp.bfloat16


def _norm_mod(x, nw, shift, scale):
    ms = jnp.mean(x * x, axis=-1, keepdims=True)
    y = x * lax.rsqrt(ms + EPS) * nw
    return y * (1.0 + scale) + shift


HALO = 8
assert CONV_K == 5
Z0 = 2 * F_DIM
XBC0 = Z0 + D_SSM
DT0 = XBC0 + CONV_DIM
CONV_COLS = 512
CONV_ROWS = 128
MM_COLS = 256


def _softplus(v):
    return jnp.maximum(v, 0.0) + jnp.log1p(jnp.exp(-jnp.abs(v)))


def _inproj_kernel(x_ref, xp_ref, xn_ref, sh_ref, sc_ref, nw_ref, w_ref, cw_ref, cb_ref, dtb_ref,
                   *refs, emit_fz):
    n_conv = CONV_DIM // CONV_COLS
    s_refs = refs[-n_conv:]
    if emit_fz:
        p_ref, q_ref, z_ref, xs_ref, bc_ref, dt_ref = refs[:-n_conv]
    else:
        xs_ref, bc_ref, dt_ref = refs[:-n_conv]
    i = pl.program_id(1)
    tm = x_ref.shape[1]
    nw, sh, sc = nw_ref[...], sh_ref[0], sc_ref[0]
    h = _norm_mod(x_ref[0], nw, sh, sc).astype(BF16)
    halo = jnp.concatenate([xp_ref[0], xn_ref[0]], axis=0)
    hh = _norm_mod(halo, nw, sh, sc).astype(BF16)
    dt_raw = jnp.dot(h, w_ref[:, DT0:DT0 + DT_PAD], preferred_element_type=F32)
    dt_ref[0] = _softplus(dt_raw + dtb_ref[...])

    def preconv(ci, half):
        c = half * MM_COLS
        wc = w_ref[:, XBC0 + ci * CONV_COLS + c:XBC0 + ci * CONV_COLS + c + MM_COLS]
        s_ref = s_refs[ci]
        s_ref[HALO:HALO + tm, c:c + MM_COLS] = jnp.dot(h, wc, preferred_element_type=F32)
        hal = jnp.dot(hh, wc, preferred_element_type=F32)
        s_ref[0:HALO, c:c + MM_COLS] = jnp.where(i > 0, hal[:HALO], 0.0)
        s_ref[HALO + tm:, c:c + MM_COLS] = jnp.where(
            i < pl.num_programs(1) - 1, hal[HALO:], 0.0)

    def plain(o_ref, oc, wc0):
        o_ref[0, :, oc:oc + MM_COLS] = jnp.dot(
            h, w_ref[:, wc0:wc0 + MM_COLS], preferred_element_type=F32).astype(BF16)

    def conv(ci, r0):
        c0 = ci * CONV_COLS
        n = CONV_ROWS + 2 * HALO
        xw = s_refs[ci][pl.ds(r0, n), :]
        t = [cw_ref[k:k + 1, c0:c0 + CONV_COLS] * xw for k in range(CONV_K)]
        up = lambda v: pltpu.roll(v, n - 1, 0)
        down = lambda v: pltpu.roll(v, 1, 0)
        y = t[2] + up(t[3] + up(t[4])) + down(t[1] + down(t[0]))
        acc = y[HALO:HALO + CONV_ROWS] + cb_ref[:, c0:c0 + CONV_COLS]
        act = (acc * jax.nn.sigmoid(acc)).astype(BF16)
        if c0 < D_SSM:
            xs_ref[0, r0:r0 + CONV_ROWS, c0:c0 + CONV_COLS] = act
        else:
            bc_ref[0, r0:r0 + CONV_ROWS, c0 - D_SSM:c0 - D_SSM + CONV_COLS] = act

    n_conv = CONV_DIM // CONV_COLS
    halves = CONV_COLS // MM_COLS
    mm = []
    for ci in range(n_conv):
        if ci + 1 < n_conv:
            mm += [functools.partial(preconv, ci + 1, hf) for hf in range(halves)]
    if emit_fz:
        extra = ([(p_ref, c, c) for c in range(0, F_DIM, MM_COLS)]
                 + [(q_ref, c, F_DIM + c) for c in range(0, F_DIM, MM_COLS)]
                 + [(z_ref, c, Z0 + c) for c in range(0, D_SSM, MM_COLS)])
        per = -(-len(extra) // n_conv)
        merged = []
        for ci in range(n_conv):
            merged += mm[ci * halves:(ci + 1) * halves]
            merged += [functools.partial(plain, *e) for e in extra[ci * per:(ci + 1) * per]]
        mm = merged
    for hf in range(halves):
        preconv(0, hf)
    blocks = [(ci, r0) for ci in range(n_conv) for r0 in range(0, tm, CONV_ROWS)]
    per_block = -(-len(mm) // len(blocks))
    for bi, (ci, r0) in enumerate(blocks):
        for piece in mm[bi * per_block:(bi + 1) * per_block]:
            piece()
        conv(ci, r0)


def _inproj(x, shift, scale, nw, w, conv_w, conv_b, dt_bias, emit_fz, tm):
    bsz, L, D = x.shape
    n_mod = shift.shape[0]
    mod_map = (lambda b, i: (b, 0, 0)) if n_mod > 1 else (lambda b, i: (0, 0, 0))
    const = lambda b, i: (0, 0)
    row = lambda n: pl.BlockSpec((1, tm, n), lambda b, i: (b, i, 0))
    hb = tm // HALO
    widths = ([F_DIM, F_DIM, D_SSM] if emit_fz else []) + [D_SSM, 2 * SSM_GROUPS * D_STATE, DT_PAD]
    dtypes = ([BF16, BF16, BF16] if emit_fz else []) + [BF16, BF16, F32]
    return pl.pallas_call(
        functools.partial(_inproj_kernel, emit_fz=emit_fz),
        grid=(bsz, L // tm),
        in_specs=[
            row(D),
            pl.BlockSpec((1, HALO, D), lambda b, i: (b, jnp.maximum(i * hb - 1, 0), 0)),
            pl.BlockSpec((1, HALO, D), lambda b, i: (b, jnp.minimum((i + 1) * hb, L // HALO - 1), 0)),
            pl.BlockSpec((1, 1, D), mod_map),
            pl.BlockSpec((1, 1, D), mod_map),
            pl.BlockSpec((1, D), const),
            pl.BlockSpec(w.shape, const),
            pl.BlockSpec(conv_w.shape, const),
            pl.BlockSpec(conv_b.shape, const),
            pl.BlockSpec(dt_bias.shape, const),
        ],
        out_specs=[row(n) for n in widths],
        out_shape=[jax.ShapeDtypeStruct((bsz, L, n), dt) for n, dt in zip(widths, dtypes)],
        scratch_shapes=[pltpu.VMEM((tm + 2 * HALO, CONV_COLS), F32)] * (CONV_DIM // CONV_COLS),
        compiler_params=pltpu.CompilerParams(
            dimension_semantics=("parallel", "arbitrary"),
            vmem_limit_bytes=VMEM_LIMIT),
        name="inproj",
    )(x, x, x, shift, scale, nw, w, conv_w, conv_b, dt_bias)


HEADS_PER_GROUP = SSM_HEADS // SSM_GROUPS
GROUP_W = HEADS_PER_GROUP * SSM_HEAD_DIM
PAIRS_PER_GROUP = GROUP_W // LANE


def _ssd_chunk(ldx, ldbc, dt, a_row, ex_ref, st_ref, rt_ref, reverse, emit):
    off = SSM_HEADS if reverse else 0
    end = 0 if reverse else CHUNK - 1
    r = lax.broadcasted_iota(jnp.int32, (CHUNK, CHUNK), 0)
    s = lax.broadcasted_iota(jnp.int32, (CHUNK, CHUNK), 1)
    keep = (r <= s) if reverse else (r >= s)
    left = lax.broadcasted_iota(jnp.int32, (CHUNK, LANE), 1) < SSM_HEAD_DIM
    left_bf = left.astype(BF16)
    right_bf = 1 - left_bf

    def expand_row(v, j):
        return jnp.where(left[:1], v[:, j:j + 1], v[:, j + 1:j + 2])

    dA = dt * a_row
    tri = keep.astype(BF16)
    p1 = dA.astype(BF16)
    r1 = dA - p1.astype(F32)
    p2 = r1.astype(BF16)
    p3 = (r1 - p2.astype(F32)).astype(BF16)
    acs = (jnp.dot(tri, p1, preferred_element_type=F32)
           + jnp.dot(tri, p2, preferred_element_type=F32)
           + jnp.dot(tri, p3, preferred_element_type=F32))
    yield
    acs_end = acs[end:end + 1, :]
    w_state = jnp.exp(acs_end - acs) * dt
    dec_row = jnp.exp(acs_end)
    if emit is not None:
        rt_ref[...] = (jnp.log(dt) - acs).T
        e_in = jnp.exp(acs)
    def expand(v, j):
        return jnp.take_along_axis(v, jnp.where(left, j, j + 1), axis=1,
                                   mode="promise_in_bounds")

    w_wide = jnp.dot(w_state.astype(BF16), ex_ref[...], preferred_element_type=F32)
    yield

    for g in range(SSM_GROUPS):
        bg = ldbc(D_STATE * g, D_STATE)
        cg = ldbc(SSM_GROUPS * D_STATE + D_STATE * g, D_STATE)
        hst = st_ref[g]
        if emit is not None:
            cb = lax.dot_general(cg, bg, (((1,), (1,)), ((), ())), preferred_element_type=F32)
            y_off = jnp.dot(cg, hst.astype(BF16), preferred_element_type=F32)
        xw, dec, ys = [], [], []
        for p in range(PAIRS_PER_GROUP):
            j = off + HEADS_PER_GROUP * g + 2 * p
            c0 = GROUP_W * g + LANE * p
            xp = ldx(c0, LANE)
            xw.append((xp.astype(F32) * w_wide[:, c0:c0 + LANE]).astype(BF16))
            dec.append(expand_row(dec_row, j))
            if emit is not None:
                ms = []
                for jj in (j, j + 1):
                    seg = acs[:, jj:jj + 1] + rt_ref[jj:jj + 1, :]
                    ms.append((cb * jnp.exp(jnp.where(keep, seg, -jnp.inf))).astype(BF16))
                lhs = jnp.concatenate(ms, axis=1)
                rhs = jnp.concatenate([xp * left_bf, xp * right_bf], axis=0)
                y_d = jnp.dot(lhs, rhs, preferred_element_type=F32)
                ys.append(y_d + y_off[:, LANE * p:LANE * (p + 1)] * expand(e_in, j))
            yield
        upd = lax.dot_general(bg, jnp.concatenate(xw, axis=1), (((0,), (0,)), ((), ())),
                              preferred_element_type=F32)
        st_ref[g] = hst * jnp.concatenate(dec, axis=1) + upd
        if emit is not None:
            emit(g, jnp.concatenate(ys, axis=1))
        yield


def _interleave(gens):
    gens = list(gens)
    while gens:
        for gen in list(gens):
            try:
                next(gen)
            except StopIteration:
                gens.remove(gen)


def _ssd_kernel(*refs, reverse, final):
    if final:
        (xs_ref, bc_ref, dt_ref, xsc_ref, bcc_ref, dtc_ref, ar_ref, ex_ref,
         yf_ref, z_ref, dsk_ref, gw_ref, o_ref, st_ref, rt_ref) = refs
    else:
        (xs_ref, bc_ref, dt_ref, xsc_ref, bcc_ref, dtc_ref, ar_ref, ex_ref,
         o_ref, st_ref, rt_ref) = refs
    n_seq = xs_ref.shape[0]
    n_chunks = xs_ref.shape[1] // CHUNK
    n_ctx = xsc_ref.shape[1] // CHUNK
    a_row = ar_ref[...]

    @pl.when(pl.program_id(1) == 0)
    def _():
        st_ref[...] = jnp.zeros(st_ref.shape, F32)
        for c in (range(n_ctx - 1, -1, -1) if reverse else range(n_ctx)):
            rows = pl.ds(c * CHUNK, CHUNK)
            _interleave(
                _ssd_chunk(lambda c0, w, n=n: xsc_ref[n, rows, c0:c0 + w],
                           lambda c0, w, n=n: bcc_ref[n, rows, c0:c0 + w],
                           dtc_ref[n, rows, :], a_row, ex_ref, st_ref.at[n], rt_ref.at[n],
                           reverse, None)
                for n in range(n_seq))

    def body(k, carry):
        c = (n_chunks - 1 - k) if reverse else k
        rows = pl.ds(pl.multiple_of(c * CHUNK, CHUNK), CHUNK)

        def emit(n, g, y):
            cols = slice(GROUP_W * g, GROUP_W * (g + 1))
            if final:
                y = y + yf_ref[n, rows, cols].astype(F32)
                y = y + dsk_ref[:, cols] * xs_ref[n, rows, cols].astype(F32)
                z = z_ref[n, rows, cols].astype(F32)
                y = y * (z * jax.nn.sigmoid(z))
                y = y * lax.rsqrt(jnp.mean(y * y, axis=-1, keepdims=True) + EPS) * gw_ref[:, cols]
            o_ref[n, rows, cols] = y.astype(BF16)

        _interleave(
            _ssd_chunk(lambda c0, w, n=n: xs_ref[n, rows, c0:c0 + w],
                       lambda c0, w, n=n: bc_ref[n, rows, c0:c0 + w],
                       dt_ref[n, rows, :], a_row, ex_ref, st_ref.at[n], rt_ref.at[n],
                       reverse, functools.partial(emit, n))
            for n in range(n_seq))
        return carry

    lax.fori_loop(0, n_chunks, body, 0)


def _ssd(xs, bc, dt, xs_c, bc_c, dt_c, a_row, reverse, extra, rows_per_step, n_seq):
    bsz, L, _ = xs.shape
    n_blk = L // rows_per_step
    blk = (lambda b, i: (b, n_blk - 1 - i, 0)) if reverse else (lambda b, i: (b, i, 0))
    row = lambda n: pl.BlockSpec((n_seq, rows_per_step, n), blk)
    ctx = lambda a: pl.BlockSpec((n_seq,) + a.shape[1:], lambda b, i: (b, 0, 0))
    const = lambda a: pl.BlockSpec(a.shape, lambda b, i: (0, 0))
    head_of_lane = (SSM_HEADS if reverse else 0) + jnp.arange(D_SSM) // SSM_HEAD_DIM
    ex = (jnp.arange(DT_PAD)[:, None] == head_of_lane[None, :]).astype(BF16)
    in_specs = [row(D_SSM), row(bc.shape[2]), row(DT_PAD), ctx(xs_c), ctx(bc_c), ctx(dt_c),
                const(a_row), const(ex)]
    args = [xs, bc, dt, xs_c, bc_c, dt_c, a_row, ex]
    if extra is not None:
        y_fwd, z, dsk, gw = extra
        in_specs += [row(D_SSM), row(D_SSM), const(dsk), const(gw)]
        args += [y_fwd, z, dsk, gw]
    return pl.pallas_call(
        functools.partial(_ssd_kernel, reverse=reverse, final=extra is not None),
        grid=(bsz // n_seq, n_blk),
        in_specs=in_specs,
        out_specs=row(D_SSM),
        out_shape=jax.ShapeDtypeStruct((bsz, L, D_SSM), BF16),
        scratch_shapes=[pltpu.VMEM((n_seq, SSM_GROUPS, D_STATE, GROUP_W), F32),
                        pltpu.VMEM((n_seq, DT_PAD, CHUNK), F32)],
        compiler_params=pltpu.CompilerParams(
            dimension_semantics=("parallel", "arbitrary"),
            vmem_limit_bytes=VMEM_LIMIT),
        name="ssd_bwd" if reverse else "ssd_fwd",
    )(*args)


def _outproj_kernel(f_ref, y_ref, x_ref, g1_ref, sh_ref, sc_ref, nw_ref, wo_ref, wr_ref,
                    x1_ref, h2_ref, p_ref):
    m = jnp.dot(f_ref[0], wo_ref[:F_DIM, :], preferred_element_type=F32)
    m = m + jnp.dot(y_ref[0], wo_ref[F_DIM:, :], preferred_element_type=F32)
    x1 = x_ref[0] + g1_ref[0] * m
    x1_ref[0] = x1
    h2 = _norm_mod(x1, nw_ref[...], sh_ref[0], sc_ref[0])
    h2_ref[0] = h2.astype(BF16)
    logits = jnp.dot(h2, wr_ref[...], preferred_element_type=F32)
    lane = lax.broadcasted_iota(jnp.int32, logits.shape, 1)
    logits = jnp.where(lane < N_EXPERTS, logits, -jnp.inf)
    e = jnp.exp(logits - jnp.max(logits, axis=-1, keepdims=True))
    p_ref[0] = e / jnp.sum(e, axis=-1, keepdims=True)


def _outproj(four, yssm, x, g1, sh2, sc2, nw, wo, wr, tm):
    bsz, L, D = x.shape
    row = lambda n: pl.BlockSpec((1, tm, n), lambda b, i: (b, i, 0))
    mod = pl.BlockSpec((1, 1, D), lambda b, i: (b, 0, 0))
    return pl.pallas_call(
        _outproj_kernel,
        grid=(bsz, L // tm),
        in_specs=[row(F_DIM), row(D_SSM), row(D), mod, mod, mod,
                  pl.BlockSpec((1, D), lambda b, i: (0, 0)),
                  pl.BlockSpec(wo.shape, lambda b, i: (0, 0)),
                  pl.BlockSpec(wr.shape, lambda b, i: (0, 0))],
        out_specs=[row(D), row(D), row(LANE)],
        out_shape=[jax.ShapeDtypeStruct((bsz, L, D), F32),
                   jax.ShapeDtypeStruct((bsz, L, D), BF16),
                   jax.ShapeDtypeStruct((bsz, L, LANE), F32)],
        compiler_params=pltpu.CompilerParams(
            dimension_semantics=("parallel", "parallel"),
            vmem_limit_bytes=VMEM_LIMIT),
        name="outproj",
    )(four, yssm, x, g1, sh2, sc2, nw, wo, wr)


FFN_SLABS = 2
FFN_SLAB = D_EXPERT // FFN_SLABS
FFN_COLS = 512


def _ffn_kernel(x_ref, wg_ref, wu_ref, wd_ref, o_ref, wgb_ref, wub_ref, wdb_ref, acc_ref):
    b, f = pl.program_id(1), pl.program_id(2)

    @pl.when(b == 0)
    def _():
        wgb_ref[f] = wg_ref[0].astype(BF16)
        wub_ref[f] = wu_ref[0].astype(BF16)
        wdb_ref[f] = wd_ref[0].astype(BF16)

    x = x_ref[0, 0]
    part = jnp.zeros(acc_ref.shape, F32)
    for c in range(0, FFN_SLAB, FFN_COLS):
        g = jnp.dot(x, wgb_ref[f, :, c:c + FFN_COLS], preferred_element_type=F32)
        u = jnp.dot(x, wub_ref[f, :, c:c + FFN_COLS], preferred_element_type=F32)
        a = (g * jax.nn.sigmoid(g) * u).astype(BF16)
        part = part + jnp.dot(a, wdb_ref[f, c:c + FFN_COLS, :], preferred_element_type=F32)

    @pl.when(f == 0)
    def _():
        acc_ref[...] = part

    @pl.when(f > 0)
    def _():
        acc_ref[...] += part

    @pl.when(f == FFN_SLABS - 1)
    def _():
        o_ref[0, 0] = acc_ref[...].astype(o_ref.dtype)


def _expert_ffn(xg, wg, wu, wd):
    bsz, n_e, cap, D = xg.shape
    slab = lambda b, f: jnp.where(b == 0, f, FFN_SLABS - 1)
    return pl.pallas_call(
        _ffn_kernel,
        grid=(n_e, bsz, FFN_SLABS),
        in_specs=[
            pl.BlockSpec((1, 1, cap, D), lambda e, b, f: (b, e, 0, 0)),
            pl.BlockSpec((1, D, FFN_SLAB), lambda e, b, f: (e, 0, slab(b, f))),
            pl.BlockSpec((1, D, FFN_SLAB), lambda e, b, f: (e, 0, slab(b, f))),
            pl.BlockSpec((1, FFN_SLAB, D), lambda e, b, f: (e, slab(b, f), 0)),
        ],
        out_specs=pl.BlockSpec((1, 1, cap, D), lambda e, b, f: (b, e, 0, 0)),
        out_shape=jax.ShapeDtypeStruct((bsz, n_e, cap, D), BF16),
        scratch_shapes=[pltpu.VMEM((FFN_SLABS, D, FFN_SLAB), BF16),
                        pltpu.VMEM((FFN_SLABS, D, FFN_SLAB), BF16),
                        pltpu.VMEM((FFN_SLABS, FFN_SLAB, D), BF16),
                        pltpu.VMEM((cap, D), F32)],
        compiler_params=pltpu.CompilerParams(
            dimension_semantics=("arbitrary", "arbitrary", "arbitrary"),
            vmem_limit_bytes=VMEM_LIMIT),
        name="expert_ffn",
    )(xg, wg, wu, wd)


def _combine_kernel(idx_ref, val_ref, y_ref, x1_ref, g_ref, nw_ref, o_ref, acc_ref):
    e = pl.program_id(2)
    tb = x1_ref.shape[1]
    n_e, cap, d = y_ref.shape[1:]

    @pl.when(e == 0)
    def _():
        acc_ref[...] = jnp.zeros(acc_ref.shape, F32)

    tok = pl.program_id(1) * tb + lax.broadcasted_iota(jnp.int32, (tb, cap), 0)
    onehot = jnp.concatenate(
        [jnp.where(tok == idx_ref[0, k], val_ref[0, k], 0.0).astype(BF16) for k in range(n_e)],
        axis=1)
    acc_ref[...] += jnp.dot(onehot, y_ref[0].reshape(n_e * cap, d), preferred_element_type=F32)

    @pl.when(e == pl.num_programs(2) - 1)
    def _():
        x = x1_ref[0] + g_ref[0] * acc_ref[...]
        ms = jnp.mean(x * x, axis=-1, keepdims=True)
        o_ref[0] = x * lax.rsqrt(ms + EPS) * nw_ref[...]


def _combine(idx, vals, y, x1, g2, nw, tb, e_blk):
    bsz, L, D = x1.shape
    n_e, cap = y.shape[1], y.shape[2]
    row = pl.BlockSpec((1, tb, D), lambda b, t, e: (b, t, 0))
    return pl.pallas_call(
        _combine_kernel,
        grid=(bsz, L // tb, n_e // e_blk),
        in_specs=[pl.BlockSpec((1, e_blk, 1, cap), lambda b, t, e: (b, e, 0, 0)),
                  pl.BlockSpec((1, e_blk, 1, cap), lambda b, t, e: (b, e, 0, 0)),
                  pl.BlockSpec((1, e_blk, cap, D), lambda b, t, e: (b, e, 0, 0)),
                  row,
                  pl.BlockSpec((1, 1, D), lambda b, t, e: (b, 0, 0)),
                  pl.BlockSpec((1, D), lambda b, t, e: (0, 0))],
        out_specs=row,
        out_shape=jax.ShapeDtypeStruct((bsz, L, D), F32),
        scratch_shapes=[pltpu.VMEM((tb, D), F32)],
        compiler_params=pltpu.CompilerParams(
            dimension_semantics=("parallel", "parallel", "arbitrary"),
            vmem_limit_bytes=VMEM_LIMIT),
        name="combine",
    )(idx, vals, y, x1, g2, nw)


DFT_BATCH = 16
DFT_PAD = 8


def _dft_kernel(p_ref, q_ref, m1h_ref, m1l_ref, m2h_ref, m2l_ref, tc_ref, ts_ref, o_ref,
                pf_ref, qf_ref, y_ref, of_ref):
    w = GRID_W
    pitch, ypitch = w + DFT_PAD, 2 * w + DFT_PAD
    for r in range(w):
        pf_ref[r * pitch:r * pitch + w, :] = p_ref[0, r * w:(r + 1) * w, :].astype(F32)
        qf_ref[r * pitch:r * pitch + w, :] = q_ref[0, r * w:(r + 1) * w, :].astype(F32)

    ch = pf_ref.shape[1]
    cols = lambda y, j: y[:, j * ch:(j + 1) * ch]

    def stage1(blk, carry):
        c0 = blk * DFT_BATCH
        v = jnp.concatenate(
            [jnp.concatenate([pf_ref[pl.ds(c0 + j, w, stride=pitch), :],
                              qf_ref[pl.ds(c0 + j, w, stride=pitch), :]], axis=0)
             for j in range(DFT_BATCH)], axis=1).astype(BF16)
        y = (jnp.dot(m1h_ref[...], v, preferred_element_type=F32)
             + jnp.dot(m1l_ref[...], v, preferred_element_type=F32))
        for j in range(DFT_BATCH):
            yr, yi = cols(y[:w], j), cols(y[w:], j)
            tc, ts = tc_ref[c0 + j], ts_ref[c0 + j]
            base = pl.multiple_of((c0 + j) * ypitch, DFT_PAD)
            y_ref[pl.ds(base, w), :] = yr * tc + yi * ts
            y_ref[pl.ds(base + w, w), :] = yi * tc - yr * ts
        return carry

    lax.fori_loop(0, w // DFT_BATCH, stage1, 0)

    def stage2(blk, carry):
        k0 = blk * DFT_BATCH
        v = jnp.concatenate(
            [jnp.concatenate([y_ref[pl.ds(k0 + j, w, stride=ypitch), :],
                              y_ref[pl.ds(w + k0 + j, w, stride=ypitch), :]], axis=0)
             for j in range(DFT_BATCH)], axis=1).astype(BF16)
        o = (jnp.dot(m2h_ref[...], v, preferred_element_type=F32)
             + jnp.dot(m2l_ref[...], v, preferred_element_type=F32))
        for j in range(DFT_BATCH):
            of_ref[pl.ds(k0 + j, w, stride=pitch), :] = cols(o, j)
        return carry

    lax.fori_loop(0, w // DFT_BATCH, stage2, 0)
    for k2 in range(w):
        o_ref[0, k2 * w:(k2 + 1) * w, :] = of_ref[k2 * pitch:k2 * pitch + w, :].astype(BF16)


def _split_bf16(m):
    hi = m.astype(BF16)
    return hi, (m - hi.astype(F32)).astype(BF16)


def _fourier_positions(p, q):
    bsz, L, _ = p.shape
    w = GRID_W
    k = jnp.arange(w, dtype=F32)
    ang = (2.0 * jnp.pi / w) * jnp.outer(k, k)
    c, s = jnp.cos(ang), jnp.sin(ang)
    m1h, m1l = _split_bf16(jnp.block([[c, -s], [-s, -c]]))
    m2h, m2l = _split_bf16(jnp.concatenate([c, s], axis=1))
    tw = (2.0 * jnp.pi / L) * jnp.outer(k, k)
    tc = jnp.broadcast_to(jnp.cos(tw)[:, :, None], (w, w, LANE))
    ts = jnp.broadcast_to(jnp.sin(tw)[:, :, None], (w, w, LANE))
    const = lambda a: pl.BlockSpec(a.shape, lambda b, i: (0,) * a.ndim)
    blk = pl.BlockSpec((1, L, LANE), lambda b, i: (b, 0, i))
    return pl.pallas_call(
        _dft_kernel,
        grid=(bsz, F_DIM // LANE),
        in_specs=[blk, blk, const(m1h), const(m1l), const(m2h), const(m2l), const(tc), const(ts)],
        out_specs=blk,
        out_shape=jax.ShapeDtypeStruct((bsz, L, F_DIM), BF16),
        scratch_shapes=[pltpu.VMEM((w * (w + DFT_PAD), LANE), F32),
                        pltpu.VMEM((w * (w + DFT_PAD), LANE), F32),
                        pltpu.VMEM((w * (2 * w + DFT_PAD), LANE), F32),
                        pltpu.VMEM((w * (w + DFT_PAD), LANE), F32)],
        compiler_params=pltpu.CompilerParams(
            dimension_semantics=("parallel", "parallel"), vmem_limit_bytes=VMEM_LIMIT),
        name="dft_positions",
    )(p, q, m1h, m1l, m2h, m2l, tc, ts)


def _fourier_fold(w_uf, w_fourier):
    n = F_GROUP_DIM
    k = jnp.arange(n, dtype=F32)
    ang = (2.0 * jnp.pi / n) * jnp.outer(k, k)
    hi = lax.Precision.HIGHEST
    scale = 1.0 / jnp.sqrt(jnp.float32(GRID_W * GRID_W * n))
    a = jnp.einsum('cm,gmj->gcj', jnp.cos(ang), w_fourier, precision=hi) * scale
    b = jnp.einsum('cm,gmj->gcj', jnp.sin(ang), w_fourier, precision=hi) * scale
    wg = w_uf.reshape(-1, F_GROUPS, n)
    wp = jnp.einsum('dgc,gcj->dgj', wg, a, precision=hi).reshape(-1, F_DIM)
    wq = jnp.einsum('dgc,gcj->dgj', wg, b, precision=hi).reshape(-1, F_DIM)
    return wp, wq


def kernel(x, c, ctx, c_ctx, w_ada, b_ada, norm1_w, w_in, conv_w, conv_b, dt_bias, a_log,
           d_skip, ssm_norm_w, w_fourier, w_out, norm2_w, w_router, w_gate, w_up, w_down,
           final_norm_w):
    bsz, L, D = x.shape
    hi = lax.Precision.HIGHEST
    mod_l = (jnp.dot(jax.nn.silu(c), w_ada[0], precision=hi) + b_ada[0]).reshape(bsz, N_MOD, 1, D)
    mod_c = (jnp.dot(jax.nn.silu(c_ctx), w_ada[0], precision=hi) + b_ada[0]).reshape(N_MOD, 1, 1, D)
    sh1, sc1, g1, sh2, sc2, g2 = [mod_l[:, i] for i in range(N_MOD)]
    csh1, csc1 = mod_c[0], mod_c[1]

    dt_pad = DT_PAD - 2 * SSM_HEADS
    assert L == GRID_W * GRID_W
    w_p, w_q = _fourier_fold(w_in[0][:, :F_DIM], w_fourier[0])
    w_in_p = jnp.concatenate(
        [w_p, w_q, w_in[0][:, F_DIM:], jnp.zeros((D, dt_pad), F32)], axis=1).astype(BF16)
    dtb_row = jnp.pad(dt_bias[0].reshape(1, 2 * SSM_HEADS), ((0, 0), (0, dt_pad)))
    a_row = jnp.pad(-jnp.exp(a_log[0]).reshape(1, 2 * SSM_HEADS), ((0, 0), (0, dt_pad)))
    nw1 = norm1_w[0][None]
    conv_b_row = conv_b[0][None]

    xs_c, bc_c, dt_c = _inproj(ctx, csh1, csc1, nw1, w_in_p, conv_w[0], conv_b_row, dtb_row,
                               emit_fz=False, tm=ctx.shape[1])
    p, q, z, xs, bc, dt = _inproj(x, sh1, sc1, nw1, w_in_p, conv_w[0], conv_b_row, dtb_row,
                                  emit_fz=True, tm=512)
    y_fwd = _ssd(xs, bc, dt, xs_c, bc_c, dt_c, a_row, False, None, rows_per_step=256, n_seq=4)
    dsk_row = jnp.repeat(d_skip[0], SSM_HEAD_DIM)[None]
    y_ssm = _ssd(xs, bc, dt, xs_c, bc_c, dt_c, a_row, True,
                 (y_fwd, z, dsk_row, ssm_norm_w[0][None]), rows_per_step=256, n_seq=4)
    four = _fourier_positions(p, q)

    w_r = jnp.pad(w_router[0], ((0, 0), (0, LANE - N_EXPERTS)))
    x1, h2, probs = _outproj(four, y_ssm, x, g1, sh2, sc2, norm2_w[0][None],
                             w_out[0].astype(BF16), w_r, tm=512)

    cap = CAPACITY_FACTOR * L // N_EXPERTS
    vals, idx = lax.top_k(jnp.swapaxes(probs[..., :N_EXPERTS], 1, 2), cap)
    bidx = jnp.arange(bsz)[:, None, None]
    xg = h2[bidx, idx]
    y = _expert_ffn(xg, w_gate[0], w_up[0], w_down[0])
    return _combine(idx[:, :, None, :], vals[:, :, None, :], y, x1, g2, final_norm_w[None],
                    tb=1024, e_blk=8)
```

```python
import functools

import jax
import jax.numpy as jnp
from jax import lax
from jax.experimental import pallas as pl
from jax.experimental.pallas import tpu as pltpu

D_MODEL = 1024
GRID_W = 64
F_GROUP_DIM = 64
F_DIM = D_MODEL // 2
F_GROUPS = F_DIM // F_GROUP_DIM
SSM_HEAD_DIM = 64
D_SSM = 3 * D_MODEL // 2
SSM_HEADS = D_SSM // SSM_HEAD_DIM
SSM_GROUPS = 4
D_STATE = 128
CONV_K = 5
CHUNK = 128
D_MIX = F_DIM + D_SSM
CONV_DIM = D_SSM + 2 * SSM_GROUPS * D_STATE
D_IN_PROJ = F_DIM + D_SSM + CONV_DIM + 2 * SSM_HEADS
N_EXPERTS = 16
CAPACITY_FACTOR = 2
D_EXPERT = 2048
N_MOD = 6
EPS = 1e-6

LANE = 128
DT_PAD = LANE
VMEM_LIMIT = 56 * 1024 * 1024

F32 = jnp.float32
BF16 = jnp.bfloat16


def _norm_mod(x, nw, shift, scale):
    ms = jnp.mean(x * x, axis=-1, keepdims=True)
    y = x * lax.rsqrt(ms + EPS) * nw
    return y * (1.0 + scale) + shift


HALO = 8
assert CONV_K == 5
Z0 = 2 * F_DIM
XBC0 = Z0 + D_SSM
DT0 = XBC0 + CONV_DIM
CONV_COLS = 512
CONV_ROWS = 128
MM_COLS = 256


def _softplus(v):
    return jnp.maximum(v, 0.0) + jnp.log1p(jnp.exp(-jnp.abs(v)))


def _inproj_kernel(x_ref, xp_ref, xn_ref, sh_ref, sc_ref, nw_ref, w_ref, cw_ref, cb_ref, dtb_ref,
                   *refs, emit_fz):
    n_conv = CONV_DIM // CONV_COLS
    s_refs = refs[-n_conv:]
    if emit_fz:
        p_ref, q_ref, z_ref, xs_ref, bc_ref, dt_ref = refs[:-n_conv]
    else:
        xs_ref, bc_ref, dt_ref = refs[:-n_conv]
    i = pl.program_id(1)
    tm = x_ref.shape[1]
    nw, sh, sc = nw_ref[...], sh_ref[0], sc_ref[0]
    h = _norm_mod(x_ref[0], nw, sh, sc).astype(BF16)
    halo = jnp.concatenate([xp_ref[0], xn_ref[0]], axis=0)
    hh = _norm_mod(halo, nw, sh, sc).astype(BF16)
    dt_raw = jnp.dot(h, w_ref[:, DT0:DT0 + DT_PAD], preferred_element_type=F32)
    dt_ref[0] = _softplus(dt_raw + dtb_ref[...])

    def preconv(ci, half):
        c = half * MM_COLS
        wc = w_ref[:, XBC0 + ci * CONV_COLS + c:XBC0 + ci * CONV_COLS + c + MM_COLS]
        s_ref = s_refs[ci]
        s_ref[HALO:HALO + tm, c:c + MM_COLS] = jnp.dot(h, wc, preferred_element_type=F32)
        hal = jnp.dot(hh, wc, preferred_element_type=F32)
        s_ref[0:HALO, c:c + MM_COLS] = jnp.where(i > 0, hal[:HALO], 0.0)
        s_ref[HALO + tm:, c:c + MM_COLS] = jnp.where(
            i < pl.num_programs(1) - 1, hal[HALO:], 0.0)

    def plain(o_ref, oc, wc0):
        o_ref[0, :, oc:oc + MM_COLS] = jnp.dot(
            h, w_ref[:, wc0:wc0 + MM_COLS], preferred_element_type=F32).astype(BF16)

    def conv(ci, r0):
        c0 = ci * CONV_COLS
        n = CONV_ROWS + 2 * HALO
        xw = s_refs[ci][pl.ds(r0, n), :]
        t = [cw_ref[k:k + 1, c0:c0 + CONV_COLS] * xw for k in range(CONV_K)]
        up = lambda v: pltpu.roll(v, n - 1, 0)
        down = lambda v: pltpu.roll(v, 1, 0)
        y = t[2] + up(t[3] + up(t[4])) + down(t[1] + down(t[0]))
        acc = y[HALO:HALO + CONV_ROWS] + cb_ref[:, c0:c0 + CONV_COLS]
        act = (acc * jax.nn.sigmoid(acc)).astype(BF16)
        if c0 < D_SSM:
            xs_ref[0, r0:r0 + CONV_ROWS, c0:c0 + CONV_COLS] = act
        else:
            bc_ref[0, r0:r0 + CONV_ROWS, c0 - D_SSM:c0 - D_SSM + CONV_COLS] = act

    n_conv = CONV_DIM // CONV_COLS
    halves = CONV_COLS // MM_COLS
    mm = []
    for ci in range(n_conv):
        if ci + 1 < n_conv:
            mm += [functools.partial(preconv, ci + 1, hf) for hf in range(halves)]
    if emit_fz:
        extra = ([(p_ref, c, c) for c in range(0, F_DIM, MM_COLS)]
                 + [(q_ref, c, F_DIM + c) for c in range(0, F_DIM, MM_COLS)]
                 + [(z_ref, c, Z0 + c) for c in range(0, D_SSM, MM_COLS)])
        per = -(-len(extra) // n_conv)
        merged = []
        for ci in range(n_conv):
            merged += mm[ci * halves:(ci + 1) * halves]
            merged += [functools.partial(plain, *e) for e in extra[ci * per:(ci + 1) * per]]
        mm = merged
    for hf in range(halves):
        preconv(0, hf)
    blocks = [(ci, r0) for ci in range(n_conv) for r0 in range(0, tm, CONV_ROWS)]
    per_block = -(-len(mm) // len(blocks))
    for bi, (ci, r0) in enumerate(blocks):
        for piece in mm[bi * per_block:(bi + 1) * per_block]:
            piece()
        conv(ci, r0)


def _inproj(x, shift, scale, nw, w, conv_w, conv_b, dt_bias, emit_fz, tm):
    bsz, L, D = x.shape
    n_mod = shift.shape[0]
    mod_map = (lambda b, i: (b, 0, 0)) if n_mod > 1 else (lambda b, i: (0, 0, 0))
    const = lambda b, i: (0, 0)
    row = lambda n: pl.BlockSpec((1, tm, n), lambda b, i: (b, i, 0))
    hb = tm // HALO
    widths = ([F_DIM, F_DIM, D_SSM] if emit_fz else []) + [D_SSM, 2 * SSM_GROUPS * D_STATE, DT_PAD]
    dtypes = ([BF16, BF16, BF16] if emit_fz else []) + [BF16, BF16, F32]
    return pl.pallas_call(
        functools.partial(_inproj_kernel, emit_fz=emit_fz),
        grid=(bsz, L // tm),
        in_specs=[
            row(D),
            pl.BlockSpec((1, HALO, D), lambda b, i: (b, jnp.maximum(i * hb - 1, 0), 0)),
            pl.BlockSpec((1, HALO, D), lambda b, i: (b, jnp.minimum((i + 1) * hb, L // HALO - 1), 0)),
            pl.BlockSpec((1, 1, D), mod_map),
            pl.BlockSpec((1, 1, D), mod_map),
            pl.BlockSpec((1, D), const),
            pl.BlockSpec(w.shape, const),
            pl.BlockSpec(conv_w.shape, const),
            pl.BlockSpec(conv_b.shape, const),
            pl.BlockSpec(dt_bias.shape, const),
        ],
        out_specs=[row(n) for n in widths],
        out_shape=[jax.ShapeDtypeStruct((bsz, L, n), dt) for n, dt in zip(widths, dtypes)],
        scratch_shapes=[pltpu.VMEM((tm + 2 * HALO, CONV_COLS), F32)] * (CONV_DIM // CONV_COLS),
        compiler_params=pltpu.CompilerParams(
            dimension_semantics=("parallel", "arbitrary"),
            vmem_limit_bytes=VMEM_LIMIT),
        name="inproj",
    )(x, x, x, shift, scale, nw, w, conv_w, conv_b, dt_bias)


HEADS_PER_GROUP = SSM_HEADS // SSM_GROUPS
GROUP_W = HEADS_PER_GROUP * SSM_HEAD_DIM
PAIRS_PER_GROUP = GROUP_W // LANE


def _ssd_chunk(ldx, ldbc, dt, a_row, ex_ref, st_ref, rt_ref, reverse, emit):
    off = SSM_HEADS if reverse else 0
    end = 0 if reverse else CHUNK - 1
    r = lax.broadcasted_iota(jnp.int32, (CHUNK, CHUNK), 0)
    s = lax.broadcasted_iota(jnp.int32, (CHUNK, CHUNK), 1)
    keep = (r <= s) if reverse else (r >= s)
    left = lax.broadcasted_iota(jnp.int32, (CHUNK, LANE), 1) < SSM_HEAD_DIM
    left_bf = left.astype(BF16)
    right_bf = 1 - left_bf

    def expand_row(v, j):
        return jnp.where(left[:1], v[:, j:j + 1], v[:, j + 1:j + 2])

    dA = dt * a_row
    tri = keep.astype(BF16)
    p1 = dA.astype(BF16)
    r1 = dA - p1.astype(F32)
    p2 = r1.astype(BF16)
    p3 = (r1 - p2.astype(F32)).astype(BF16)
    acs = (jnp.dot(tri, p1, preferred_element_type=F32)
           + jnp.dot(tri, p2, preferred_element_type=F32)
           + jnp.dot(tri, p3, preferred_element_type=F32))
    yield
    acs_end = acs[end:end + 1, :]
    w_state = jnp.exp(acs_end - acs) * dt
    dec_row = jnp.exp(acs_end)
    if emit is not None:
        rt_ref[...] = (jnp.log(dt) - acs).T
        e_in = jnp.exp(acs)
    def expand(v, j):
        return jnp.take_along_axis(v, jnp.where(left, j, j + 1), axis=1,
                                   mode="promise_in_bounds")

    w_wide = jnp.dot(w_state.astype(BF16), ex_ref[...], preferred_element_type=F32)
    yield

    for g in range(SSM_GROUPS):
        bg = ldbc(D_STATE * g, D_STATE)
        cg = ldbc(SSM_GROUPS * D_STATE + D_STATE * g, D_STATE)
        hst = st_ref[g]
        if emit is not None:
            cb = lax.dot_general(cg, bg, (((1,), (1,)), ((), ())), preferred_element_type=F32)
            y_off = jnp.dot(cg, hst.astype(BF16), preferred_element_type=F32)
        xw, dec, ys = [], [], []
        for p in range(PAIRS_PER_GROUP):
            j = off + HEADS_PER_GROUP * g + 2 * p
            c0 = GROUP_W * g + LANE * p
            xp = ldx(c0, LANE)
            xw.append((xp.astype(F32) * w_wide[:, c0:c0 + LANE]).astype(BF16))
            dec.append(expand_row(dec_row, j))
            if emit is not None:
                ms = []
                for jj in (j, j + 1):
                    seg = acs[:, jj:jj + 1] + rt_ref[jj:jj + 1, :]
                    ms.append((cb * jnp.exp(jnp.where(keep, seg, -jnp.inf))).astype(BF16))
                lhs = jnp.concatenate(ms, axis=1)
                rhs = jnp.concatenate([xp * left_bf, xp * right_bf], axis=0)
                y_d = jnp.dot(lhs, rhs, preferred_element_type=F32)
                ys.append(y_d + y_off[:, LANE * p:LANE * (p + 1)] * expand(e_in, j))
            yield
        upd = lax.dot_general(bg, jnp.concatenate(xw, axis=1), (((0,), (0,)), ((), ())),
                              preferred_element_type=F32)
        st_ref[g] = hst * jnp.concatenate(dec, axis=1) + upd
        if emit is not None:
            emit(g, jnp.concatenate(ys, axis=1))
        yield


def _interleave(gens):
    gens = list(gens)
    while gens:
        for gen in list(gens):
            try:
                next(gen)
            except StopIteration:
                gens.remove(gen)


def _ssd_kernel(*refs, reverse, final):
    if final:
        (xs_ref, bc_ref, dt_ref, xsc_ref, bcc_ref, dtc_ref, ar_ref, ex_ref,
         yf_ref, z_ref, dsk_ref, gw_ref, o_ref, st_ref, rt_ref) = refs
    else:
        (xs_ref, bc_ref, dt_ref, xsc_ref, bcc_ref, dtc_ref, ar_ref, ex_ref,
         o_ref, st_ref, rt_ref) = refs
    n_seq = xs_ref.shape[0]
    n_chunks = xs_ref.shape[1] // CHUNK
    n_ctx = xsc_ref.shape[1] // CHUNK
    a_row = ar_ref[...]

    @pl.when(pl.program_id(1) == 0)
    def _():
        st_ref[...] = jnp.zeros(st_ref.shape, F32)
        for c in (range(n_ctx - 1, -1, -1) if reverse else range(n_ctx)):
            rows = pl.ds(c * CHUNK, CHUNK)
            _interleave(
                _ssd_chunk(lambda c0, w, n=n: xsc_ref[n, rows, c0:c0 + w],
                           lambda c0, w, n=n: bcc_ref[n, rows, c0:c0 + w],
                           dtc_ref[n, rows, :], a_row, ex_ref, st_ref.at[n], rt_ref.at[n],
                           reverse, None)
                for n in range(n_seq))

    def body(k, carry):
        c = (n_chunks - 1 - k) if reverse else k
        rows = pl.ds(pl.multiple_of(c * CHUNK, CHUNK), CHUNK)

        def emit(n, g, y):
            cols = slice(GROUP_W * g, GROUP_W * (g + 1))
            if final:
                y = y + yf_ref[n, rows, cols].astype(F32)
                y = y + dsk_ref[:, cols] * xs_ref[n, rows, cols].astype(F32)
                z = z_ref[n, rows, cols].astype(F32)
                y = y * (z * jax.nn.sigmoid(z))
                y = y * lax.rsqrt(jnp.mean(y * y, axis=-1, keepdims=True) + EPS) * gw_ref[:, cols]
            o_ref[n, rows, cols] = y.astype(BF16)

        _interleave(
            _ssd_chunk(lambda c0, w, n=n: xs_ref[n, rows, c0:c0 + w],
                       lambda c0, w, n=n: bc_ref[n, rows, c0:c0 + w],
                       dt_ref[n, rows, :], a_row, ex_ref, st_ref.at[n], rt_ref.at[n],
                       reverse, functools.partial(emit, n))
            for n in range(n_seq))
        return carry

    lax.fori_loop(0, n_chunks, body, 0)


def _ssd(xs, bc, dt, xs_c, bc_c, dt_c, a_row, reverse, extra, rows_per_step, n_seq):
    bsz, L, _ = xs.shape
    n_blk = L // rows_per_step
    blk = (lambda b, i: (b, n_blk - 1 - i, 0)) if reverse else (lambda b, i: (b, i, 0))
    row = lambda n: pl.BlockSpec((n_seq, rows_per_step, n), blk)
    ctx = lambda a: pl.BlockSpec((n_seq,) + a.shape[1:], lambda b, i: (b, 0, 0))
    const = lambda a: pl.BlockSpec(a.shape, lambda b, i: (0, 0))
    head_of_lane = (SSM_HEADS if reverse else 0) + jnp.arange(D_SSM) // SSM_HEAD_DIM
    ex = (jnp.arange(DT_PAD)[:, None] == head_of_lane[None, :]).astype(BF16)
    in_specs = [row(D_SSM), row(bc.shape[2]), row(DT_PAD), ctx(xs_c), ctx(bc_c), ctx(dt_c),
                const(a_row), const(ex)]
    args = [xs, bc, dt, xs_c, bc_c, dt_c, a_row, ex]
    if extra is not None:
        y_fwd, z, dsk, gw = extra
        in_specs += [row(D_SSM), row(D_SSM), const(dsk), const(gw)]
        args += [y_fwd, z, dsk, gw]
    return pl.pallas_call(
        functools.partial(_ssd_kernel, reverse=reverse, final=extra is not None),
        grid=(bsz // n_seq, n_blk),
        in_specs=in_specs,
        out_specs=row(D_SSM),
        out_shape=jax.ShapeDtypeStruct((bsz, L, D_SSM), BF16),
        scratch_shapes=[pltpu.VMEM((n_seq, SSM_GROUPS, D_STATE, GROUP_W), F32),
                        pltpu.VMEM((n_seq, DT_PAD, CHUNK), F32)],
        compiler_params=pltpu.CompilerParams(
            dimension_semantics=("parallel", "arbitrary"),
            vmem_limit_bytes=VMEM_LIMIT),
        name="ssd_bwd" if reverse else "ssd_fwd",
    )(*args)


def _outproj_kernel(f_ref, y_ref, x_ref, g1_ref, sh_ref, sc_ref, nw_ref, wo_ref, wr_ref,
                    x1_ref, h2_ref, p_ref):
    m = jnp.dot(f_ref[0], wo_ref[:F_DIM, :], preferred_element_type=F32)
    m = m + jnp.dot(y_ref[0], wo_ref[F_DIM:, :], preferred_element_type=F32)
    x1 = x_ref[0] + g1_ref[0] * m
    x1_ref[0] = x1
    h2 = _norm_mod(x1, nw_ref[...], sh_ref[0], sc_ref[0])
    h2_ref[0] = h2.astype(BF16)
    logits = jnp.dot(h2, wr_ref[...], preferred_element_type=F32)
    lane = lax.broadcasted_iota(jnp.int32, logits.shape, 1)
    logits = jnp.where(lane < N_EXPERTS, logits, -jnp.inf)
    e = jnp.exp(logits - jnp.max(logits, axis=-1, keepdims=True))
    p_ref[0] = e / jnp.sum(e, axis=-1, keepdims=True)


def _outproj(four, yssm, x, g1, sh2, sc2, nw, wo, wr, tm):
    bsz, L, D = x.shape
    row = lambda n: pl.BlockSpec((1, tm, n), lambda b, i: (b, i, 0))
    mod = pl.BlockSpec((1, 1, D), lambda b, i: (b, 0, 0))
    return pl.pallas_call(
        _outproj_kernel,
        grid=(bsz, L // tm),
        in_specs=[row(F_DIM), row(D_SSM), row(D), mod, mod, mod,
                  pl.BlockSpec((1, D), lambda b, i: (0, 0)),
                  pl.BlockSpec(wo.shape, lambda b, i: (0, 0)),
                  pl.BlockSpec(wr.shape, lambda b, i: (0, 0))],
        out_specs=[row(D), row(D), row(LANE)],
        out_shape=[jax.ShapeDtypeStruct((bsz, L, D), F32),
                   jax.ShapeDtypeStruct((bsz, L, D), BF16),
                   jax.ShapeDtypeStruct((bsz, L, LANE), F32)],
        compiler_params=pltpu.CompilerParams(
            dimension_semantics=("parallel", "parallel"),
            vmem_limit_bytes=VMEM_LIMIT),
        name="outproj",
    )(four, yssm, x, g1, sh2, sc2, nw, wo, wr)


FFN_SLABS = 2
FFN_SLAB = D_EXPERT // FFN_SLABS
FFN_COLS = 1024


def _ffn_kernel(x_ref, wg_ref, wu_ref, wd_ref, o_ref, wgb_ref, wub_ref, wdb_ref, acc_ref):
    b, f = pl.program_id(1), pl.program_id(2)

    @pl.when(b == 0)
    def _():
        wgb_ref[f] = wg_ref[0].astype(BF16)
        wub_ref[f] = wu_ref[0].astype(BF16)
        wdb_ref[f] = wd_ref[0].astype(BF16)

    x = x_ref[0, 0]
    part = jnp.zeros(acc_ref.shape, F32)
    for c in range(0, FFN_SLAB, FFN_COLS):
        g = jnp.dot(x, wgb_ref[f, :, c:c + FFN_COLS], preferred_element_type=F32)
        u = jnp.dot(x, wub_ref[f, :, c:c + FFN_COLS], preferred_element_type=F32)
        a = (g * jax.nn.sigmoid(g) * u).astype(BF16)
        part = part + jnp.dot(a, wdb_ref[f, c:c + FFN_COLS, :], preferred_element_type=F32)

    @pl.when(f == 0)
    def _():
        acc_ref[...] = part

    @pl.when(f > 0)
    def _():
        acc_ref[...] += part

    @pl.when(f == FFN_SLABS - 1)
    def _():
        o_ref[0, 0] = acc_ref[...].astype(o_ref.dtype)


def _expert_ffn(xg, wg, wu, wd):
    bsz, n_e, cap, D = xg.shape
    slab = lambda b, f: jnp.where(b == 0, f, FFN_SLABS - 1)
    return pl.pallas_call(
        _ffn_kernel,
        grid=(n_e, bsz, FFN_SLABS),
        in_specs=[
            pl.BlockSpec((1, 1, cap, D), lambda e, b, f: (b, e, 0, 0)),
            pl.BlockSpec((1, D, FFN_SLAB), lambda e, b, f: (e, 0, slab(b, f))),
            pl.BlockSpec((1, D, FFN_SLAB), lambda e, b, f: (e, 0, slab(b, f))),
            pl.BlockSpec((1, FFN_SLAB, D), lambda e, b, f: (e, slab(b, f), 0)),
        ],
        out_specs=pl.BlockSpec((1, 1, cap, D), lambda e, b, f: (b, e, 0, 0)),
        out_shape=jax.ShapeDtypeStruct((bsz, n_e, cap, D), BF16),
        scratch_shapes=[pltpu.VMEM((FFN_SLABS, D, FFN_SLAB), BF16),
                        pltpu.VMEM((FFN_SLABS, D, FFN_SLAB), BF16),
                        pltpu.VMEM((FFN_SLABS, FFN_SLAB, D), BF16),
                        pltpu.VMEM((cap, D), F32)],
        compiler_params=pltpu.CompilerParams(
            dimension_semantics=("arbitrary", "arbitrary", "arbitrary"),
            vmem_limit_bytes=VMEM_LIMIT),
        name="expert_ffn",
    )(xg, wg, wu, wd)


def _combine_kernel(idx_ref, val_ref, y_ref, x1_ref, g_ref, nw_ref, o_ref, acc_ref):
    e = pl.program_id(2)
    tb = x1_ref.shape[1]
    n_e, cap, d = y_ref.shape[1:]

    @pl.when(e == 0)
    def _():
        acc_ref[...] = jnp.zeros(acc_ref.shape, F32)

    tok = pl.program_id(1) * tb + lax.broadcasted_iota(jnp.int32, (tb, cap), 0)
    onehot = jnp.concatenate(
        [jnp.where(tok == idx_ref[0, k], val_ref[0, k], 0.0).astype(BF16) for k in range(n_e)],
        axis=1)
    acc_ref[...] += jnp.dot(onehot, y_ref[0].reshape(n_e * cap, d), preferred_element_type=F32)

    @pl.when(e == pl.num_programs(2) - 1)
    def _():
        x = x1_ref[0] + g_ref[0] * acc_ref[...]
        ms = jnp.mean(x * x, axis=-1, keepdims=True)
        o_ref[0] = x * lax.rsqrt(ms + EPS) * nw_ref[...]


def _combine(idx, vals, y, x1, g2, nw, tb, e_blk):
    bsz, L, D = x1.shape
    n_e, cap = y.shape[1], y.shape[2]
    row = pl.BlockSpec((1, tb, D), lambda b, t, e: (b, t, 0))
    return pl.pallas_call(
        _combine_kernel,
        grid=(bsz, L // tb, n_e // e_blk),
        in_specs=[pl.BlockSpec((1, e_blk, 1, cap), lambda b, t, e: (b, e, 0, 0)),
                  pl.BlockSpec((1, e_blk, 1, cap), lambda b, t, e: (b, e, 0, 0)),
                  pl.BlockSpec((1, e_blk, cap, D), lambda b, t, e: (b, e, 0, 0)),
                  row,
                  pl.BlockSpec((1, 1, D), lambda b, t, e: (b, 0, 0)),
                  pl.BlockSpec((1, D), lambda b, t, e: (0, 0))],
        out_specs=row,
        out_shape=jax.ShapeDtypeStruct((bsz, L, D), F32),
        scratch_shapes=[pltpu.VMEM((tb, D), F32)],
        compiler_params=pltpu.CompilerParams(
            dimension_semantics=("parallel", "parallel", "arbitrary"),
            vmem_limit_bytes=VMEM_LIMIT),
        name="combine",
    )(idx, vals, y, x1, g2, nw)


DFT_BATCH = 16
DFT_PAD = 8


def _dft_kernel(p_ref, q_ref, m1h_ref, m1l_ref, m2h_ref, m2l_ref, tc_ref, ts_ref, o_ref,
                pf_ref, qf_ref, y_ref, of_ref):
    w = GRID_W
    pitch, ypitch = w + DFT_PAD, 2 * w + DFT_PAD
    for r in range(w):
        pf_ref[r * pitch:r * pitch + w, :] = p_ref[0, r * w:(r + 1) * w, :].astype(F32)
        qf_ref[r * pitch:r * pitch + w, :] = q_ref[0, r * w:(r + 1) * w, :].astype(F32)

    ch = pf_ref.shape[1]
    cols = lambda y, j: y[:, j * ch:(j + 1) * ch]

    def stage1(blk, carry):
        c0 = blk * DFT_BATCH
        v = jnp.concatenate(
            [jnp.concatenate([pf_ref[pl.ds(c0 + j, w, stride=pitch), :],
                              qf_ref[pl.ds(c0 + j, w, stride=pitch), :]], axis=0)
             for j in range(DFT_BATCH)], axis=1).astype(BF16)
        y = (jnp.dot(m1h_ref[...], v, preferred_element_type=F32)
             + jnp.dot(m1l_ref[...], v, preferred_element_type=F32))
        for j in range(DFT_BATCH):
            yr, yi = cols(y[:w], j), cols(y[w:], j)
            tc, ts = tc_ref[c0 + j], ts_ref[c0 + j]
            base = pl.multiple_of((c0 + j) * ypitch, DFT_PAD)
            y_ref[pl.ds(base, w), :] = yr * tc + yi * ts
            y_ref[pl.ds(base + w, w), :] = yi * tc - yr * ts
        return carry

    lax.fori_loop(0, w // DFT_BATCH, stage1, 0)

    def stage2(blk, carry):
        k0 = blk * DFT_BATCH
        v = jnp.concatenate(
            [jnp.concatenate([y_ref[pl.ds(k0 + j, w, stride=ypitch), :],
                              y_ref[pl.ds(w + k0 + j, w, stride=ypitch), :]], axis=0)
             for j in range(DFT_BATCH)], axis=1).astype(BF16)
        o = (jnp.dot(m2h_ref[...], v, preferred_element_type=F32)
             + jnp.dot(m2l_ref[...], v, preferred_element_type=F32))
        for j in range(DFT_BATCH):
            of_ref[pl.ds(k0 + j, w, stride=pitch), :] = cols(o, j)
        return carry

    lax.fori_loop(0, w // DFT_BATCH, stage2, 0)
    for k2 in range(w):
        o_ref[0, k2 * w:(k2 + 1) * w, :] = of_ref[k2 * pitch:k2 * pitch + w, :].astype(BF16)


def _split_bf16(m):
    hi = m.astype(BF16)
    return hi, (m - hi.astype(F32)).astype(BF16)


def _fourier_positions(p, q):
    bsz, L, _ = p.shape
    w = GRID_W
    k = jnp.arange(w, dtype=F32)
    ang = (2.0 * jnp.pi / w) * jnp.outer(k, k)
    c, s = jnp.cos(ang), jnp.sin(ang)
    m1h, m1l = _split_bf16(jnp.block([[c, -s], [-s, -c]]))
    m2h, m2l = _split_bf16(jnp.concatenate([c, s], axis=1))
    tw = (2.0 * jnp.pi / L) * jnp.outer(k, k)
    tc = jnp.broadcast_to(jnp.cos(tw)[:, :, None], (w, w, LANE))
    ts = jnp.broadcast_to(jnp.sin(tw)[:, :, None], (w, w, LANE))
    const = lambda a: pl.BlockSpec(a.shape, lambda b, i: (0,) * a.ndim)
    blk = pl.BlockSpec((1, L, LANE), lambda b, i: (b, 0, i))
    return pl.pallas_call(
        _dft_kernel,
        grid=(bsz, F_DIM // LANE),
        in_specs=[blk, blk, const(m1h), const(m1l), const(m2h), const(m2l), const(tc), const(ts)],
        out_specs=blk,
        out_shape=jax.ShapeDtypeStruct((bsz, L, F_DIM), BF16),
        scratch_shapes=[pltpu.VMEM((w * (w + DFT_PAD), LANE), F32),
                        pltpu.VMEM((w * (w + DFT_PAD), LANE), F32),
                        pltpu.VMEM((w * (2 * w + DFT_PAD), LANE), F32),
                        pltpu.VMEM((w * (w + DFT_PAD), LANE), F32)],
        compiler_params=pltpu.CompilerParams(
            dimension_semantics=("parallel", "parallel"), vmem_limit_bytes=VMEM_LIMIT),
        name="dft_positions",
    )(p, q, m1h, m1l, m2h, m2l, tc, ts)


def _fourier_fold(w_uf, w_fourier):
    n = F_GROUP_DIM
    k = jnp.arange(n, dtype=F32)
    ang = (2.0 * jnp.pi / n) * jnp.outer(k, k)
    hi = lax.Precision.HIGHEST
    scale = 1.0 / jnp.sqrt(jnp.float32(GRID_W * GRID_W * n))
    a = jnp.einsum('cm,gmj->gcj', jnp.cos(ang), w_fourier, precision=hi) * scale
    b = jnp.einsum('cm,gmj->gcj', jnp.sin(ang), w_fourier, precision=hi) * scale
    wg = w_uf.reshape(-1, F_GROUPS, n)
    wp = jnp.einsum('dgc,gcj->dgj', wg, a, precision=hi).reshape(-1, F_DIM)
    wq = jnp.einsum('dgc,gcj->dgj', wg, b, precision=hi).reshape(-1, F_DIM)
    return wp, wq


def kernel(x, c, ctx, c_ctx, w_ada, b_ada, norm1_w, w_in, conv_w, conv_b, dt_bias, a_log,
           d_skip, ssm_norm_w, w_fourier, w_out, norm2_w, w_router, w_gate, w_up, w_down,
           final_norm_w):
    bsz, L, D = x.shape
    hi = lax.Precision.HIGHEST
    mod_l = (jnp.dot(jax.nn.silu(c), w_ada[0], precision=hi) + b_ada[0]).reshape(bsz, N_MOD, 1, D)
    mod_c = (jnp.dot(jax.nn.silu(c_ctx), w_ada[0], precision=hi) + b_ada[0]).reshape(N_MOD, 1, 1, D)
    sh1, sc1, g1, sh2, sc2, g2 = [mod_l[:, i] for i in range(N_MOD)]
    csh1, csc1 = mod_c[0], mod_c[1]

    dt_pad = DT_PAD - 2 * SSM_HEADS
    assert L == GRID_W * GRID_W
    w_p, w_q = _fourier_fold(w_in[0][:, :F_DIM], w_fourier[0])
    w_in_p = jnp.concatenate(
        [w_p, w_q, w_in[0][:, F_DIM:], jnp.zeros((D, dt_pad), F32)], axis=1).astype(BF16)
    dtb_row = jnp.pad(dt_bias[0].reshape(1, 2 * SSM_HEADS), ((0, 0), (0, dt_pad)))
    a_row = jnp.pad(-jnp.exp(a_log[0]).reshape(1, 2 * SSM_HEADS), ((0, 0), (0, dt_pad)))
    nw1 = norm1_w[0][None]
    conv_b_row = conv_b[0][None]

    xs_c, bc_c, dt_c = _inproj(ctx, csh1, csc1, nw1, w_in_p, conv_w[0], conv_b_row, dtb_row,
                               emit_fz=False, tm=ctx.shape[1])
    p, q, z, xs, bc, dt = _inproj(x, sh1, sc1, nw1, w_in_p, conv_w[0], conv_b_row, dtb_row,
                                  emit_fz=True, tm=512)
    y_fwd = _ssd(xs, bc, dt, xs_c, bc_c, dt_c, a_row, False, None, rows_per_step=256, n_seq=4)
    dsk_row = jnp.repeat(d_skip[0], SSM_HEAD_DIM)[None]
    y_ssm = _ssd(xs, bc, dt, xs_c, bc_c, dt_c, a_row, True,
                 (y_fwd, z, dsk_row, ssm_norm_w[0][None]), rows_per_step=256, n_seq=4)
    four = _fourier_positions(p, q)

    w_r = jnp.pad(w_router[0], ((0, 0), (0, LANE - N_EXPERTS)))
    x1, h2, probs = _outproj(four, y_ssm, x, g1, sh2, sc2, norm2_w[0][None],
                             w_out[0].astype(BF16), w_r, tm=1024)

    cap = CAPACITY_FACTOR * L // N_EXPERTS
    vals, idx = lax.top_k(jnp.swapaxes(probs[..., :N_EXPERTS], 1, 2), cap)
    bidx = jnp.arange(bsz)[:, None, None]
    xg = h2[bidx, idx]
    y = _expert_ffn(xg, w_gate[0], w_up[0], w_down[0])
    return _combine(idx[:, :, None, :], vals[:, :, None, :], y, x1, g2, final_norm_w[None],
                    tb=1024, e_blk=8)
```
